```python
import jax, jax.numpy as jnp
from jax import lax
import numpy as np

D_MODEL = 1024
BATCH = 8
SEQ = 2048
DEPTH = 4
DEC_BATCH = 128
DEC_SEQ = 1
PAST_LEN = 8192
PAGE_SIZE = 128

N_MIXERS = 2
N_CONV_LAYERS = (DEPTH + 1) // 2
N_ATTN_LAYERS = DEPTH // 2
CONV_WIDTH = 3
N_HEADS = 8
QK_NOPE_DIM = 128
QK_ROPE_DIM = 64
V_HEAD_DIM = 128
Q_LORA_RANK = 384
KV_LORA_RANK = 256
D_FF = 2816
ROPE_THETA = 10000.0
RMS_EPS = 1e-6
Q_BLOCK = 128
N_MOD = 6
ATTN_SCALE = (QK_NOPE_DIM + QK_ROPE_DIM) ** -0.5

kernel_name = 'hybrid_shortconv_mla_convffn_step'


def rmsnorm(x, g):
    xf = x.astype(jnp.float32)
    y = xf * lax.rsqrt(jnp.mean(xf * xf, axis=-1, keepdims=True) + RMS_EPS)
    return (y * g.astype(jnp.float32)).astype(x.dtype)


def rope(x, pos):
    half = QK_ROPE_DIM // 2
    inv = ROPE_THETA ** (-jnp.arange(half, dtype=jnp.float32) * 2.0 / QK_ROPE_DIM)
    ang = pos.astype(jnp.float32)[:, None] * inv[None, :]
    ang = ang.reshape((ang.shape[0],) + (1,) * (x.ndim - 3) + (half,))
    cos, sin = jnp.cos(ang), jnp.sin(ang)
    xf = x.astype(jnp.float32)
    x1, x2 = xf[..., :half], xf[..., half:]
    return jnp.concatenate([x1 * cos - x2 * sin, x2 * cos + x1 * sin], axis=-1).astype(x.dtype)


def causal_dwconv(x, prev, w):
    T = x.shape[1]
    xp = jnp.concatenate([prev.astype(x.dtype), x], axis=1)
    y = xp[:, 0:T] * w[0]
    for k in range(1, CONV_WIDTH):
        y = y + xp[:, k:k + T] * w[k]
    return y, xp[:, T:]


def short_conv_mixer(h, prev, w_in, conv_w, w_out):
    b_gate, c_gate, v = jnp.split(h @ w_in, 3, axis=-1)
    y, buf = causal_dwconv(c_gate * v, prev, conv_w)
    return (b_gate * y) @ w_out, buf


def conv_ffn(h, prev, w_in, conv_w, conv_b, w_out):
    a, v = jnp.split(h @ w_in, 2, axis=-1)
    a, buf = causal_dwconv(a, prev, conv_w)
    return (jax.nn.silu(a + conv_b) * v) @ w_out, buf


def mla_attend_block(q_lat, q_pe, kv_lat, k_pe, q_pos, k_pos):
    s = (jnp.einsum('bthc,bsc->bhts', q_lat, kv_lat, preferred_element_type=jnp.float32)
         + jnp.einsum('bthr,bsr->bhts', q_pe, k_pe, preferred_element_type=jnp.float32)) * ATTN_SCALE
    mask = k_pos[None, :] <= q_pos[:, None]
    s = jnp.where(mask[None, None], s, jnp.finfo(jnp.float32).min)
    p = jax.nn.softmax(s, axis=-1).astype(kv_lat.dtype)
    return jnp.einsum('bhts,bsc->bthc', p, kv_lat)


def mla_attend(q_lat, q_pe, kv_lat, k_pe, q_pos, k_pos):
    B, T = q_lat.shape[0], q_lat.shape[1]
    qb = Q_BLOCK if T % Q_BLOCK == 0 else T
    nb = T // qb
    if nb == 1:
        return mla_attend_block(q_lat, q_pe, kv_lat, k_pe, q_pos, k_pos)

    def split(a):
        return a.reshape((B, nb, qb) + a.shape[2:]).swapaxes(0, 1)

    out = lax.map(lambda a: mla_attend_block(a[0], a[1], kv_lat, k_pe, a[2], k_pos),
                  (split(q_lat), split(q_pe), q_pos.reshape(nb, qb)))
    return out.swapaxes(0, 1).reshape((B, T) + out.shape[3:])


def mla_mixer(h, pos, past_lat, past_pe, w_qa, qa_norm, w_qb, w_kva, kv_norm, w_kvb, w_o):
    B, T, _ = h.shape
    q = (rmsnorm(h @ w_qa, qa_norm) @ w_qb).reshape(B, T, N_HEADS, QK_NOPE_DIM + QK_ROPE_DIM)
    q_nope, q_pe = q[..., :QK_NOPE_DIM], rope(q[..., QK_NOPE_DIM:], pos)
    kv = h @ w_kva
    lat = rmsnorm(kv[..., :KV_LORA_RANK], kv_norm)
    pe = rope(kv[..., KV_LORA_RANK:], pos)
    w_kvb_h = w_kvb.reshape(KV_LORA_RANK, N_HEADS, QK_NOPE_DIM + V_HEAD_DIM)
    q_lat = jnp.einsum('bthn,chn->bthc', q_nope, w_kvb_h[..., :QK_NOPE_DIM])
    if past_lat is None:
        keys_lat, keys_pe = lat, pe
    else:
        keys_lat = jnp.concatenate([past_lat.astype(lat.dtype), lat], axis=1)
        keys_pe = jnp.concatenate([past_pe.astype(pe.dtype), pe], axis=1)
    k_pos = jnp.arange(keys_lat.shape[1], dtype=jnp.int32)
    o_lat = mla_attend(q_lat, q_pe, keys_lat, keys_pe, pos, k_pos)
    o = jnp.einsum('bthc,chv->bthv', o_lat, w_kvb_h[..., QK_NOPE_DIM:]).reshape(B, T, N_HEADS * V_HEAD_DIM)
    return o @ w_o, lat, pe


def trunk(x, c, pos, mix_prev, ffn_prev, attn_past, p):
    B = x.shape[0]
    new_lat, new_pe, new_mix, new_ffn = [], [], [], []
    for i in range(DEPTH):
        j = i // N_MIXERS
        m = (jax.nn.silu(c) @ p['mod_w'][i] + p['mod_b'][i]).reshape(B, 1, N_MOD, D_MODEL)
        sh1, sc1, g1, sh2, sc2, g2 = (m[:, :, k] for k in range(N_MOD))
        h = rmsnorm(x, p['norm_mix_pre'][i]) * (1 + sc1) + sh1
        if i % N_MIXERS == 0:
            o, buf = short_conv_mixer(h, mix_prev[j], p['sc_w_in'][j], p['sc_conv_w'][j], p['sc_w_out'][j])
            new_mix.append(buf)
        else:
            if attn_past is None:
                past_lat, past_pe = None, None
            else:
                cache_lat, cache_pe, page_table = attn_past
                past_lat = cache_lat[j, page_table].reshape(B, -1, KV_LORA_RANK)
                past_pe = cache_pe[j, page_table].reshape(B, -1, QK_ROPE_DIM)
            o, lat, pe = mla_mixer(h, pos, past_lat, past_pe, p['mla_w_qa'][j], p['mla_qa_norm'][j],
                                   p['mla_w_qb'][j], p['mla_w_kva'][j], p['mla_kv_norm'][j],
                                   p['mla_w_kvb'][j], p['mla_w_o'][j])
            new_lat.append(lat)
            new_pe.append(pe)
        x = x + g1 * rmsnorm(o, p['norm_mix_post'][i])
        h = rmsnorm(x, p['norm_ffn_pre'][i]) * (1 + sc2) + sh2
        o, buf = conv_ffn(h, ffn_prev[i], p['ffn_w_in'][i], p['ffn_conv_w'][i], p['ffn_conv_b'][i], p['ffn_w_out'][i])
        new_ffn.append(buf)
        x = x + g2 * rmsnorm(o, p['norm_ffn_post'][i])
    return x, jnp.stack(new_lat), jnp.stack(new_pe), jnp.stack(new_mix), jnp.stack(new_ffn)


def setup_inputs(seed: int = 0) -> dict:
    key = jax.random.key(seed)
    ks = iter(jax.random.split(key, 40))
    f32 = jnp.float32

    def nrm(shape, scale=1.0):
        return jax.random.normal(next(ks), shape, f32) * scale

    n_pages = PAST_LEN // PAGE_SIZE
    n_used = DEC_BATCH * n_pages
    n_pool = n_used + n_used // 4
    page_table = jax.random.permutation(next(ks), n_pool)[:n_used].reshape(DEC_BATCH, n_pages).astype(jnp.int32)
    dq = N_HEADS * (QK_NOPE_DIM + QK_ROPE_DIM)
    dkv = N_HEADS * (QK_NOPE_DIM + V_HEAD_DIM)
    return {
        'x_prompt': nrm((BATCH, SEQ, D_MODEL)),
        'x_sample': nrm((DEC_BATCH, DEC_SEQ, D_MODEL)),
        'cache_kv_latent': nrm((N_ATTN_LAYERS, n_pool, PAGE_SIZE, KV_LORA_RANK)),
        'cache_k_rope': nrm((N_ATTN_LAYERS, n_pool, PAGE_SIZE, QK_ROPE_DIM)),
        'state_mixconv': nrm((N_CONV_LAYERS, DEC_BATCH, CONV_WIDTH - 1, D_MODEL)),
        'state_ffnconv': nrm((DEPTH, DEC_BATCH, CONV_WIDTH - 1, D_FF)),
        'page_table': page_table,
        'c_prompt': nrm((BATCH, D_MODEL)),
        'c_sample': nrm((DEC_BATCH, D_MODEL)),
        'mod_w': nrm((DEPTH, D_MODEL, N_MOD * D_MODEL), 0.5 * D_MODEL ** -0.5),
        'mod_b': nrm((DEPTH, N_MOD * D_MODEL), 0.02),
        'norm_mix_pre': 1.0 + nrm((DEPTH, D_MODEL), 0.02),
        'norm_mix_post': 1.0 + nrm((DEPTH, D_MODEL), 0.02),
        'norm_ffn_pre': 1.0 + nrm((DEPTH, D_MODEL), 0.02),
        'norm_ffn_post': 1.0 + nrm((DEPTH, D_MODEL), 0.02),
        'sc_w_in': nrm((N_CONV_LAYERS, D_MODEL, 3 * D_MODEL), D_MODEL ** -0.5),
        'sc_conv_w': nrm((N_CONV_LAYERS, CONV_WIDTH, D_MODEL), CONV_WIDTH ** -0.5),
        'sc_w_out': nrm((N_CONV_LAYERS, D_MODEL, D_MODEL), D_MODEL ** -0.5),
        'mla_w_qa': nrm((N_ATTN_LAYERS, D_MODEL, Q_LORA_RANK), D_MODEL ** -0.5),
        'mla_qa_norm': 1.0 + nrm((N_ATTN_LAYERS, Q_LORA_RANK), 0.02),
        'mla_w_qb': nrm((N_ATTN_LAYERS, Q_LORA_RANK, dq), Q_LORA_RANK ** -0.5),
        'mla_w_kva': nrm((N_ATTN_LAYERS, D_MODEL, KV_LORA_RANK + QK_ROPE_DIM), D_MODEL ** -0.5),
        'mla_kv_norm': 1.0 + nrm((N_ATTN_LAYERS, KV_LORA_RANK), 0.02),
        'mla_w_kvb': nrm((N_ATTN_LAYERS, KV_LORA_RANK, dkv), KV_LORA_RANK ** -0.5),
        'mla_w_o': nrm((N_ATTN_LAYERS, N_HEADS * V_HEAD_DIM, D_MODEL), (N_HEADS * V_HEAD_DIM) ** -0.5),
        'ffn_w_in': nrm((DEPTH, D_MODEL, 2 * D_FF), D_MODEL ** -0.5),
        'ffn_conv_w': nrm((DEPTH, CONV_WIDTH, D_FF), CONV_WIDTH ** -0.5),
        'ffn_conv_b': nrm((DEPTH, D_FF), 0.02),
        'ffn_w_out': nrm((DEPTH, D_FF, D_MODEL), D_FF ** -0.5),
    }


def reference(x_prompt, x_sample, cache_kv_latent, cache_k_rope, state_mixconv, state_ffnconv, page_table,
              c_prompt, c_sample, mod_w, mod_b, norm_mix_pre, norm_mix_post, norm_ffn_pre, norm_ffn_post,
              sc_w_in, sc_conv_w, sc_w_out, mla_w_qa, mla_qa_norm, mla_w_qb, mla_w_kva, mla_kv_norm,
              mla_w_kvb, mla_w_o, ffn_w_in, ffn_conv_w, ffn_conv_b, ffn_w_out):
    p = {'mod_w': mod_w, 'mod_b': mod_b, 'norm_mix_pre': norm_mix_pre, 'norm_mix_post': norm_mix_post,
         'norm_ffn_pre': norm_ffn_pre, 'norm_ffn_post': norm_ffn_post, 'sc_w_in': sc_w_in,
         'sc_conv_w': sc_conv_w, 'sc_w_out': sc_w_out, 'mla_w_qa': mla_w_qa, 'mla_qa_norm': mla_qa_norm,
         'mla_w_qb': mla_w_qb, 'mla_w_kva': mla_w_kva, 'mla_kv_norm': mla_kv_norm, 'mla_w_kvb': mla_w_kvb,
         'mla_w_o': mla_w_o, 'ffn_w_in': ffn_w_in, 'ffn_conv_w': ffn_conv_w, 'ffn_conv_b': ffn_conv_b,
         'ffn_w_out': ffn_w_out}
    B, S = x_prompt.shape[0], x_prompt.shape[1]
    pos_p = jnp.arange(S, dtype=jnp.int32)
    mix0 = jnp.zeros((N_CONV_LAYERS, B, CONV_WIDTH - 1, D_MODEL), x_prompt.dtype)
    ffn0 = jnp.zeros((DEPTH, B, CONV_WIDTH - 1, D_FF), x_prompt.dtype)
    y_prompt, lat_p, pe_p, mix_p, ffn_p = trunk(x_prompt, c_prompt, pos_p, mix0, ffn0, None, p)
    past_len = page_table.shape[1] * cache_kv_latent.shape[2]
    pos_s = past_len + jnp.arange(x_sample.shape[1], dtype=jnp.int32)
    y_sample, lat_s, pe_s, mix_s, ffn_s = trunk(x_sample, c_sample, pos_s, state_mixconv, state_ffnconv,
                                                (cache_kv_latent, cache_k_rope, page_table), p)
    return (y_prompt, y_sample, lat_p, pe_p, lat_s, pe_s, mix_p, mix_s, ffn_p, ffn_s)
```

```python
import functools

import jax
import jax.numpy as jnp
from jax import lax
from jax.experimental import pallas as pl
from jax.experimental.pallas import tpu as pltpu

F32 = jnp.float32
BF16 = jnp.bfloat16

D_MODEL = 1024
N_HEADS = 8
QK_NOPE_DIM = 128
QK_ROPE_DIM = 64
ROPE_HALF = QK_ROPE_DIM // 2
V_HEAD_DIM = 128
Q_LORA_RANK = 384
KV_LORA_RANK = 256
ROPE_PAD = 128
QK_DIM = KV_LORA_RANK + ROPE_PAD
D_FF = 2816
ROPE_THETA = 10000.0
RMS_EPS = 1e-6
N_MOD = 6
ATTN_SCALE = (QK_NOPE_DIM + QK_ROPE_DIM) ** -0.5
PAGE_SIZE = 128
NEG_BIG = float(jnp.finfo(jnp.float32).min)

V7X_VMEM_LIMIT_BYTES = 56 * 1024 * 1024
TOKEN_TILE = 512
HIDDEN_CHUNK = 256
ATTN_TQ = 256
ATTN_TK = 512
DEC_HEAD_ROWS = 16


def _dot(a, b):
    return jnp.dot(a, b, preferred_element_type=F32)


def _dot_nt(a, b):
    return lax.dot_general(a, b, (((1,), (1,)), ((), ())), preferred_element_type=F32)


def _rms(x, g):
    return x * lax.rsqrt(jnp.mean(x * x, axis=-1, keepdims=True) + RMS_EPS) * g


def _silu(x):
    return x / (1.0 + jnp.exp(-x))


def _resident(shape):
    return pl.BlockSpec(shape, lambda *_: (0,) * len(shape), pipeline_mode=pl.Buffered(1))


def _mod_specs(seq_mode, tile):
    if seq_mode:
        return pl.BlockSpec((None, N_MOD, D_MODEL), lambda b, t: (b, 0, 0))
    return pl.BlockSpec((tile, N_MOD * D_MODEL), lambda b, t: (t, 0))


def _mod_get(mod_ref, k, seq_mode):
    if seq_mode:
        return mod_ref[k:k + 1, :]
    return mod_ref[:, k * D_MODEL:(k + 1) * D_MODEL]


def _mod_kernel(c_ref, w_ref, b_ref, o_ref):
    a = _silu(c_ref[...]).astype(BF16)
    o_ref[...] = _dot(a, w_ref[...].astype(BF16)) + b_ref[...]


def _modulation(c_all, mod_w, mod_b):
    depth, _, n_out = mod_w.shape
    rows = c_all.shape[0]
    tn = 1536
    return pl.pallas_call(
        _mod_kernel,
        out_shape=jax.ShapeDtypeStruct((depth, rows, n_out), F32),
        grid=(depth, n_out // tn),
        in_specs=[pl.BlockSpec((rows, D_MODEL), lambda l, n: (0, 0)),
                  pl.BlockSpec((None, D_MODEL, tn), lambda l, n: (l, 0, n)),
                  pl.BlockSpec((None, 1, tn), lambda l, n: (l, 0, n))],
        out_specs=pl.BlockSpec((None, rows, tn), lambda l, n: (l, 0, n)),
        compiler_params=pltpu.CompilerParams(dimension_semantics=("parallel", "parallel"),
                                             vmem_limit_bytes=V7X_VMEM_LIMIT_BYTES),
        name="adaln_modulation",
    )(c_all, mod_w, mod_b.reshape(depth, 1, n_out))


def _gmlp_kernel(*refs, kind, seq_mode, tile, hidden):
    if seq_mode:
        (x_ref, mod_ref, npre_ref, npost_ref, win_ref, cw_ref, cb_ref, wout_ref,
         xo_ref, st_ref, h_scr, acc_scr, halo_scr) = refs
    else:
        (x_ref, mod_ref, npre_ref, npost_ref, win_ref, cw_ref, cb_ref, wout_ref, p0_ref, p1_ref,
         xo_ref, st_ref, h_scr, acc_scr) = refs
    mo = 0 if kind == "mixer" else 3
    shift, scale, gate = (_mod_get(mod_ref, mo + k, seq_mode) for k in range(3))

    x = x_ref[...]
    h_scr[...] = (_rms(x, npre_ref[...]) * (1.0 + scale) + shift).astype(BF16)

    if seq_mode:
        @pl.when(pl.program_id(1) == 0)
        def _():
            halo_scr[...] = jnp.zeros_like(halo_scr)
        row = lax.broadcasted_iota(jnp.int32, (tile, 1), 0)

    ck = HIDDEN_CHUNK
    for c in range(hidden // ck):
        lo = c * ck
        hb = h_scr[...]
        if kind == "mixer":
            bg = _dot(hb, win_ref[:, lo:lo + ck])
            u = _dot(hb, win_ref[:, hidden + lo:hidden + lo + ck]) * _dot(hb, win_ref[:, 2 * hidden + lo:2 * hidden + lo + ck])
        else:
            u = _dot(hb, win_ref[:, lo:lo + ck])
            vg = _dot(hb, win_ref[:, hidden + lo:hidden + lo + ck])
        if seq_mode:
            h0 = halo_scr[0:1, lo:lo + ck]
            h1 = halo_scr[1:2, lo:lo + ck]
            u1 = jnp.where(row == 0, h1, pltpu.roll(u, 1, 0))
            u2 = jnp.where(row == 0, h0, jnp.where(row == 1, h1, pltpu.roll(u, 2, 0)))
            last2 = u[tile - 2:tile, :]
            halo_scr[:, lo:lo + ck] = last2
            st_ref[:, lo:lo + ck] = last2
        else:
            u1 = p1_ref[:, lo:lo + ck]
            u2 = p0_ref[:, lo:lo + ck]
            st_ref[:, lo:lo + ck] = u
        y = u2 * cw_ref[0:1, lo:lo + ck] + u1 * cw_ref[1:2, lo:lo + ck] + u * cw_ref[2:3, lo:lo + ck]
        if kind == "mixer":
            g = bg * y
        else:
            g = _silu(y + cb_ref[:, lo:lo + ck]) * vg
        part = _dot(g.astype(BF16), wout_ref[lo:lo + ck, :])
        if c == 0:
            acc_scr[...] = part
        else:
            acc_scr[...] += part

    xo_ref[...] = x_ref[...] + gate * _rms(acc_scr[...], npost_ref[...])


def _gmlp(x, mod, npre, npost, w_in, conv_w, conv_b, w_out, prev, *, kind, seq_mode):
    nb, length, _ = x.shape
    hidden = w_out.shape[0]
    tile = TOKEN_TILE if seq_mode else length
    grid = (nb, length // tile)
    x_spec = pl.BlockSpec((None, tile, D_MODEL), lambda b, t: (b, t, 0))
    in_specs = [x_spec, _mod_specs(seq_mode, tile),
                _resident((1, D_MODEL)), _resident((1, D_MODEL)),
                _resident(w_in.shape), _resident((3, hidden)), _resident((1, hidden)), _resident(w_out.shape)]
    args = [x, mod, npre.reshape(1, -1), npost.reshape(1, -1), w_in, conv_w, conv_b.reshape(1, -1), w_out]
    scratch = [pltpu.VMEM((tile, D_MODEL), BF16), pltpu.VMEM((tile, D_MODEL), F32)]
    if seq_mode:
        st_shape = jax.ShapeDtypeStruct((nb, 2, hidden), F32)
        st_spec = pl.BlockSpec((None, 2, hidden), lambda b, t: (b, 0, 0))
        scratch.append(pltpu.VMEM((2, hidden), F32))
    else:
        prev_spec = pl.BlockSpec((tile, hidden), lambda b, t: (t, 0))
        in_specs += [prev_spec, prev_spec]
        args += [prev[:, 0], prev[:, 1]]
        st_shape = jax.ShapeDtypeStruct((length, hidden), F32)
        st_spec = pl.BlockSpec((tile, hidden), lambda b, t: (t, 0))
    return pl.pallas_call(
        functools.partial(_gmlp_kernel, kind=kind, seq_mode=seq_mode, tile=tile, hidden=hidden),
        out_shape=(jax.ShapeDtypeStruct(x.shape, F32), st_shape),
        grid=grid,
        in_specs=in_specs,
        out_specs=(x_spec, st_spec),
        scratch_shapes=scratch,
        compiler_params=pltpu.CompilerParams(dimension_semantics=("parallel", "arbitrary"),
                                             vmem_limit_bytes=V7X_VMEM_LIMIT_BYTES),
        name=f"gated_conv_mlp_{kind}_{'seq' if seq_mode else 'tok'}",
    )(*args)


def _mla_proj_kernel(x_ref, mod_ref, npre_ref, wqa_ref, qan_ref, wqb_ref, wkva_ref, kvn_ref, wkt_ref,
                     cos_ref, sin_ref, q_out, kc_out, lat_out, pe_out, *, seq_mode):
    shift, scale = (_mod_get(mod_ref, k, seq_mode) for k in range(2))
    h = (_rms(x_ref[...], npre_ref[...]) * (1.0 + scale) + shift).astype(BF16)
    cos = cos_ref[...]
    sin = sin_ref[...]

    kv = _dot(h, wkva_ref[...])
    lat = _rms(kv[:, :KV_LORA_RANK], kvn_ref[...])
    pe = kv[:, KV_LORA_RANK:KV_LORA_RANK + ROPE_PAD] * cos + kv[:, KV_LORA_RANK + ROPE_PAD:] * sin
    lat_out[...] = lat
    pe_out[...] = pe[:, :QK_ROPE_DIM]
    kc_out[:, :KV_LORA_RANK] = lat.astype(BF16)
    kc_out[:, KV_LORA_RANK:] = pe.astype(BF16)

    qa = _rms(_dot(h, wqa_ref[...]), qan_ref[...]).astype(BF16)
    q = _dot(qa, wqb_ref[...])
    nope_w = N_HEADS * QK_NOPE_DIM
    pe_w = N_HEADS * ROPE_PAD
    for hd in range(N_HEADS):
        qn = q[:, hd * QK_NOPE_DIM:(hd + 1) * QK_NOPE_DIM].astype(BF16)
        ql = _dot(qn, wkt_ref[hd]) * ATTN_SCALE
        qp = (q[:, nope_w + hd * ROPE_PAD:nope_w + (hd + 1) * ROPE_PAD] * cos
              + q[:, nope_w + pe_w + hd * ROPE_PAD:nope_w + pe_w + (hd + 1) * ROPE_PAD] * sin) * ATTN_SCALE
        q_out[hd, :, :KV_LORA_RANK] = ql.astype(BF16)
        q_out[hd, :, KV_LORA_RANK:] = qp.astype(BF16)


def _mla_proj(x, mod, npre, w_qa, qa_norm, w_qb, w_kva, kv_norm, w_kt, cos, sin, *, seq_mode):
    nb, length, _ = x.shape
    tile = TOKEN_TILE if seq_mode else length
    row_spec = lambda w: pl.BlockSpec((None, tile, w), lambda b, t: (b, t, 0))
    tab_spec = pl.BlockSpec((tile, ROPE_PAD), lambda b, t: (t, 0))
    return pl.pallas_call(
        functools.partial(_mla_proj_kernel, seq_mode=seq_mode),
        out_shape=(jax.ShapeDtypeStruct((nb, N_HEADS, length, QK_DIM), BF16),
                   jax.ShapeDtypeStruct((nb, length, QK_DIM), BF16),
                   jax.ShapeDtypeStruct((nb, length, KV_LORA_RANK), F32),
                   jax.ShapeDtypeStruct((nb, length, QK_ROPE_DIM), F32)),
        grid=(nb, length // tile),
        in_specs=[row_spec(D_MODEL), _mod_specs(seq_mode, tile), _resident((1, D_MODEL)),
                  _resident(w_qa.shape), _resident((1, Q_LORA_RANK)), _resident(w_qb.shape),
                  _resident(w_kva.shape), _resident((1, KV_LORA_RANK)), _resident(w_kt.shape),
                  tab_spec, tab_spec],
        out_specs=(pl.BlockSpec((None, N_HEADS, tile, QK_DIM), lambda b, t: (b, 0, t, 0)),
                   row_spec(QK_DIM), row_spec(KV_LORA_RANK), row_spec(QK_ROPE_DIM)),
        compiler_params=pltpu.CompilerParams(dimension_semantics=("parallel", "parallel"),
                                             vmem_limit_bytes=V7X_VMEM_LIMIT_BYTES),
        name=f"mla_proj_{'seq' if seq_mode else 'tok'}",
    )(x, mod, npre.reshape(1, -1), w_qa, qa_norm.reshape(1, -1), w_qb, w_kva, kv_norm.reshape(1, -1), w_kt,
      cos, sin)


def _attn_kernel(q_ref, k_ref, o_ref, m_scr, l_scr, acc_scr):
    qi = pl.program_id(1)
    kj = pl.program_id(2)
    rows = N_HEADS * ATTN_TQ
    last = ((qi + 1) * ATTN_TQ - 1) // ATTN_TK

    @pl.when(kj == 0)
    def _():
        m_scr[...] = jnp.full_like(m_scr, NEG_BIG)
        l_scr[...] = jnp.zeros_like(l_scr)
        acc_scr[...] = jnp.zeros_like(acc_scr)

    def step(masked):
        q = q_ref[...].reshape(rows, QK_DIM)
        k = k_ref[...]
        s = _dot_nt(q, k)
        if masked:
            qpos = qi * ATTN_TQ + (lax.broadcasted_iota(jnp.int32, (rows, 1), 0) & (ATTN_TQ - 1))
            kpos = kj * ATTN_TK + lax.broadcasted_iota(jnp.int32, (1, ATTN_TK), 1)
            s = jnp.where(kpos <= qpos, s, NEG_BIG)
        m_prev = m_scr[...]
        m_new = jnp.maximum(m_prev, jnp.max(s, axis=-1, keepdims=True))
        alpha = jnp.exp(m_prev - m_new)
        p = jnp.exp(s - m_new)
        l_scr[...] = alpha * l_scr[...] + jnp.sum(p, axis=-1, keepdims=True)
        acc_scr[...] = alpha * acc_scr[...] + _dot(p.astype(BF16), k[:, :KV_LORA_RANK])
        m_scr[...] = m_new

    @pl.when(kj < last)
    def _():
        step(False)

    @pl.when(kj == last)
    def _():
        step(True)
        o = acc_scr[...] / l_scr[...]
        o_ref[...] = o.reshape(N_HEADS, ATTN_TQ, KV_LORA_RANK).astype(BF16)


def _attention(q, kc):
    nb, _, length, _ = q.shape
    rows = N_HEADS * ATTN_TQ

    def k_index(b, i, j):
        return (b, jnp.minimum(j, ((i + 1) * ATTN_TQ - 1) // ATTN_TK), 0)

    return pl.pallas_call(
        _attn_kernel,
        out_shape=jax.ShapeDtypeStruct((nb, N_HEADS, length, KV_LORA_RANK), BF16),
        grid=(nb, length // ATTN_TQ, length // ATTN_TK),
        in_specs=[pl.BlockSpec((None, N_HEADS, ATTN_TQ, QK_DIM), lambda b, i, j: (b, 0, i, 0)),
                  pl.BlockSpec((None, ATTN_TK, QK_DIM), k_index)],
        out_specs=pl.BlockSpec((None, N_HEADS, ATTN_TQ, KV_LORA_RANK), lambda b, i, j: (b, 0, i, 0)),
        scratch_shapes=[pltpu.VMEM((rows, 1), F32), pltpu.VMEM((rows, 1), F32),
                        pltpu.VMEM((rows, KV_LORA_RANK), F32)],
        compiler_params=pltpu.CompilerParams(dimension_semantics=("parallel", "parallel", "arbitrary"),
                                             vmem_limit_bytes=V7X_VMEM_LIMIT_BYTES),
        name="mla_causal_attention",
    )(q, kc)


def _page_copies(pt_ref, lat_hbm, pe_hbm, lat_buf, pe_buf, sems, seq, slot, page, n_pages):
    src = pt_ref[seq * n_pages + page]
    dst = pl.ds(pl.multiple_of(page * PAGE_SIZE, PAGE_SIZE), PAGE_SIZE)
    return (pltpu.make_async_copy(lat_hbm.at[src], lat_buf.at[slot, dst, :], sems.at[0, slot]),
            pltpu.make_async_copy(pe_hbm.at[src], pe_buf.at[slot, dst, :], sems.at[1, slot]))


def _dec_attn_kernel(pt_ref, q_ref, kn_ref, lat_hbm, pe_hbm, o_ref, lat_buf, pe_buf, sems, *, n_pages):
    i = pl.program_id(0)
    slot = i % 2

    def start_all(seq, dst_slot):
        def body(p, carry):
            for cp in _page_copies(pt_ref, lat_hbm, pe_hbm, lat_buf, pe_buf, sems, seq, dst_slot, p, n_pages):
                cp.start()
            return carry
        lax.fori_loop(0, n_pages, body, 0)

    @pl.when(i == 0)
    def _():
        start_all(0, 0)

    @pl.when(i + 1 < pl.num_programs(0))
    def _():
        start_all(i + 1, 1 - slot)

    def wait_body(p, carry):
        for cp in _page_copies(pt_ref, lat_hbm, pe_hbm, lat_buf, pe_buf, sems, i, slot, p, n_pages):
            cp.wait()
        return carry
    lax.fori_loop(0, n_pages, wait_body, 0)

    q = q_ref[...]
    kn = kn_ref[...].astype(F32)
    lat = lat_buf[slot].astype(BF16)
    pe = pe_buf[slot].astype(BF16)
    s = _dot_nt(q[:, :KV_LORA_RANK], lat) + _dot_nt(q[:, KV_LORA_RANK:KV_LORA_RANK + QK_ROPE_DIM], pe)
    s_new = jnp.sum(q.astype(F32) * kn, axis=-1, keepdims=True)
    m = jnp.maximum(jnp.max(s, axis=-1, keepdims=True), s_new)
    p = jnp.exp(s - m)
    p_new = jnp.exp(s_new - m)
    denom = jnp.sum(p, axis=-1, keepdims=True) + p_new
    o = _dot(p.astype(BF16), lat) + p_new * kn[:, :KV_LORA_RANK]
    o_ref[...] = (o / denom).astype(BF16)


def _decode_attention(q, kn, cache_lat, cache_pe, page_table):
    n_seq, n_pages = page_table.shape
    past = n_pages * PAGE_SIZE
    grid_spec = pltpu.PrefetchScalarGridSpec(
        num_scalar_prefetch=1,
        grid=(n_seq,),
        in_specs=[pl.BlockSpec((None, DEC_HEAD_ROWS, QK_DIM), lambda i, pt: (i, 0, 0)),
                  pl.BlockSpec((None, 1, QK_DIM), lambda i, pt: (i, 0, 0)),
                  pl.BlockSpec(memory_space=pl.ANY),
                  pl.BlockSpec(memory_space=pl.ANY)],
        out_specs=pl.BlockSpec((None, DEC_HEAD_ROWS, KV_LORA_RANK), lambda i, pt: (i, 0, 0)),
        scratch_shapes=[pltpu.VMEM((2, past, KV_LORA_RANK), F32),
                        pltpu.VMEM((2, past, QK_ROPE_DIM), F32),
                        pltpu.SemaphoreType.DMA((2, 2))],
    )
    return pl.pallas_call(
        functools.partial(_dec_attn_kernel, n_pages=n_pages),
        out_shape=jax.ShapeDtypeStruct((n_seq, DEC_HEAD_ROWS, KV_LORA_RANK), BF16),
        grid_spec=grid_spec,
        compiler_params=pltpu.CompilerParams(dimension_semantics=("arbitrary",),
                                             vmem_limit_bytes=V7X_VMEM_LIMIT_BYTES),
        name="mla_paged_decode_attention",
    )(page_table.reshape(-1), q, kn, cache_lat, cache_pe)


def _mla_out_kernel(ol_ref, x_ref, mod_ref, npost_ref, wv_ref, wo_ref, xo_ref, ov_scr, *, seq_mode):
    gate = _mod_get(mod_ref, 2, seq_mode)
    for hd in range(N_HEADS):
        ov_scr[:, hd * V_HEAD_DIM:(hd + 1) * V_HEAD_DIM] = _dot(ol_ref[hd], wv_ref[hd]).astype(BF16)
    o = _dot(ov_scr[...], wo_ref[...])
    xo_ref[...] = x_ref[...] + gate * _rms(o, npost_ref[...])


def _mla_out(o_lat, x, mod, npost, w_v, w_o, *, seq_mode):
    nb, length, _ = x.shape
    tile = TOKEN_TILE if seq_mode else length
    x_spec = pl.BlockSpec((None, tile, D_MODEL), lambda b, t: (b, t, 0))
    return pl.pallas_call(
        functools.partial(_mla_out_kernel, seq_mode=seq_mode),
        out_shape=jax.ShapeDtypeStruct(x.shape, F32),
        grid=(nb, length // tile),
        in_specs=[pl.BlockSpec((None, N_HEADS, tile, KV_LORA_RANK), lambda b, t: (b, 0, t, 0)),
                  x_spec, _mod_specs(seq_mode, tile), _resident((1, D_MODEL)),
                  _resident(w_v.shape), _resident(w_o.shape)],
        out_specs=x_spec,
        scratch_shapes=[pltpu.VMEM((tile, N_HEADS * V_HEAD_DIM), BF16)],
        compiler_params=pltpu.CompilerParams(dimension_semantics=("parallel", "parallel"),
                                             vmem_limit_bytes=V7X_VMEM_LIMIT_BYTES),
        name=f"mla_out_{'seq' if seq_mode else 'tok'}",
    )(o_lat, x, mod, npost.reshape(1, -1), w_v, w_o)


def _pad_lanes(w, width):
    return jnp.pad(w, [(0, 0)] * (w.ndim - 1) + [(0, width - w.shape[-1])])


def _swap_halves(w):
    return jnp.concatenate([w[..., ROPE_HALF:], w[..., :ROPE_HALF]], axis=-1)


def _prep_mla_weights(w_qb, w_kva, w_kvb):
    qb = w_qb.reshape(Q_LORA_RANK, N_HEADS, QK_NOPE_DIM + QK_ROPE_DIM)
    q_nope = qb[..., :QK_NOPE_DIM].reshape(Q_LORA_RANK, -1)
    q_pe = qb[..., QK_NOPE_DIM:]
    w_qb2 = jnp.concatenate([q_nope,
                             _pad_lanes(q_pe, ROPE_PAD).reshape(Q_LORA_RANK, -1),
                             _pad_lanes(_swap_halves(q_pe), ROPE_PAD).reshape(Q_LORA_RANK, -1)], axis=-1)
    k_pe = w_kva[:, KV_LORA_RANK:]
    w_kva2 = jnp.concatenate([w_kva[:, :KV_LORA_RANK], _pad_lanes(k_pe, ROPE_PAD),
                              _pad_lanes(_swap_halves(k_pe), ROPE_PAD)], axis=-1)
    kvb = w_kvb.reshape(KV_LORA_RANK, N_HEADS, QK_NOPE_DIM + V_HEAD_DIM)
    w_kt = jnp.transpose(kvb[..., :QK_NOPE_DIM], (1, 2, 0))
    w_v = jnp.transpose(kvb[..., QK_NOPE_DIM:], (1, 0, 2))
    return w_qb2.astype(BF16), w_kva2.astype(BF16), w_kt.astype(BF16), w_v.astype(BF16)


def _rope_tables(pos):
    inv = ROPE_THETA ** (-jnp.arange(ROPE_HALF, dtype=F32) * 2.0 / QK_ROPE_DIM)
    ang = pos.astype(F32)[:, None] * inv[None, :]
    cos, sin = jnp.cos(ang), jnp.sin(ang)
    return (_pad_lanes(jnp.concatenate([cos, cos], axis=-1), ROPE_PAD),
            _pad_lanes(jnp.concatenate([-sin, sin], axis=-1), ROPE_PAD))


def _trunk(x, mods, pos, mix_prev, ffn_prev, attn_past, w, *, seq_mode):
    depth = len(mods)
    cos, sin = _rope_tables(pos)
    new_lat, new_pe, new_mix, new_ffn = [], [], [], []
    for i in range(depth):
        j = i // 2
        mod = mods[i]
        if i % 2 == 0:
            x, st = _gmlp(x, mod, w["norm_mix_pre"][i], w["norm_mix_post"][i], w["sc_w_in"][j], w["sc_conv_w"][j],
                          jnp.zeros((D_MODEL,), F32), w["sc_w_out"][j],
                          None if seq_mode else mix_prev[j], kind="mixer", seq_mode=seq_mode)
            new_mix.append(st if seq_mode else jnp.stack([mix_prev[j][:, 1], st], axis=1))
        else:
            q, kc, lat, pe = _mla_proj(x, mod, w["norm_mix_pre"][i], w["mla_w_qa"][j], w["mla_qa_norm"][j],
                                       w["mla_w_qb"][j], w["mla_w_kva"][j], w["mla_kv_norm"][j], w["mla_w_kt"][j],
                                       cos, sin, seq_mode=seq_mode)
            if seq_mode:
                o_lat = _attention(q, kc)
            else:
                cache_lat, cache_pe, page_table = attn_past
                qs = jnp.pad(jnp.transpose(q[0], (1, 0, 2)), ((0, 0), (0, DEC_HEAD_ROWS - N_HEADS), (0, 0)))
                o = _decode_attention(qs, jnp.transpose(kc, (1, 0, 2)), cache_lat[j], cache_pe[j], page_table)
                o_lat = jnp.transpose(o[:, :N_HEADS], (1, 0, 2))[None]
            x = _mla_out(o_lat, x, mod, w["norm_mix_post"][i], w["mla_w_v"][j], w["mla_w_o"][j], seq_mode=seq_mode)
            new_lat.append(lat)
            new_pe.append(pe)
        x, st = _gmlp(x, mod, w["norm_ffn_pre"][i], w["norm_ffn_post"][i], w["ffn_w_in"][i], w["ffn_conv_w"][i],
                      w["ffn_conv_b"][i], w["ffn_w_out"][i],
                      None if seq_mode else ffn_prev[i], kind="ffn", seq_mode=seq_mode)
        new_ffn.append(st if seq_mode else jnp.stack([ffn_prev[i][:, 1], st], axis=1))
    return x, jnp.stack(new_lat), jnp.stack(new_pe), jnp.stack(new_mix), jnp.stack(new_ffn)


def kernel(x_prompt, x_sample, cache_kv_latent, cache_k_rope, state_mixconv, state_ffnconv, page_table, c_prompt, c_sample, mod_w, mod_b, norm_mix_pre, norm_mix_post, norm_ffn_pre, norm_ffn_post, sc_w_in, sc_conv_w, sc_w_out, mla_w_qa, mla_qa_norm, mla_w_qb, mla_w_kva, mla_kv_norm, mla_w_kvb, mla_w_o, ffn_w_in, ffn_conv_w, ffn_conv_b, ffn_w_out):
    depth = mod_w.shape[0]
    n_attn = mla_w_qb.shape[0]
    n_prompt, seq_len, _ = x_prompt.shape
    n_sample, dec_len, _ = x_sample.shape
    assert dec_len == 1 and seq_len % TOKEN_TILE == 0 and seq_len % ATTN_TK == 0

    prepped = [_prep_mla_weights(mla_w_qb[j], mla_w_kva[j], mla_w_kvb[j]) for j in range(n_attn)]
    w = {
        "norm_mix_pre": norm_mix_pre, "norm_mix_post": norm_mix_post,
        "norm_ffn_pre": norm_ffn_pre, "norm_ffn_post": norm_ffn_post,
        "sc_w_in": sc_w_in.astype(BF16), "sc_conv_w": sc_conv_w, "sc_w_out": sc_w_out.astype(BF16),
        "mla_w_qa": mla_w_qa.astype(BF16), "mla_qa_norm": mla_qa_norm, "mla_kv_norm": mla_kv_norm,
        "mla_w_qb": [p[0] for p in prepped], "mla_w_kva": [p[1] for p in prepped],
        "mla_w_kt": [p[2] for p in prepped], "mla_w_v": [p[3] for p in prepped],
        "mla_w_o": mla_w_o.astype(BF16),
        "ffn_w_in": ffn_w_in.astype(BF16), "ffn_conv_w": ffn_conv_w, "ffn_conv_b": ffn_conv_b,
        "ffn_w_out": ffn_w_out.astype(BF16),
    }

    mod_all = _modulation(jnp.concatenate([c_sample, c_prompt], axis=0), mod_w, mod_b)
    mods_s = [mod_all[i, :n_sample] for i in range(depth)]
    mods_p = [mod_all[i, n_sample:].reshape(n_prompt, N_MOD, D_MODEL) for i in range(depth)]

    pos_p = jnp.arange(seq_len, dtype=jnp.int32)
    y_p, lat_p, pe_p, mix_p, ffn_p = _trunk(x_prompt, mods_p, pos_p, None, None, None, w, seq_mode=True)

    past_len = page_table.shape[1] * cache_kv_latent.shape[2]
    pos_s = jnp.full((n_sample,), past_len, dtype=jnp.int32)
    xs = x_sample.reshape(1, n_sample, D_MODEL)
    y_s, lat_s, pe_s, mix_s, ffn_s = _trunk(xs, mods_s, pos_s, state_mixconv, state_ffnconv,
                                            (cache_kv_latent, cache_k_rope, page_table), w, seq_mode=False)

    return (y_p, y_s.reshape(n_sample, 1, D_MODEL), lat_p, pe_p,
            lat_s.reshape(n_attn, n_sample, 1, KV_LORA_RANK), pe_s.reshape(n_attn, n_sample, 1, QK_ROPE_DIM),
            mix_p, mix_s, ffn_p, ffn_s)
```

```python
import functools

import jax
import jax.numpy as jnp
from jax import lax
from jax.experimental import pallas as pl
from jax.experimental.pallas import tpu as pltpu

F32 = jnp.float32
BF16 = jnp.bfloat16

D_MODEL = 1024
N_HEADS = 8
QK_NOPE_DIM = 128
QK_ROPE_DIM = 64
ROPE_HALF = QK_ROPE_DIM // 2
V_HEAD_DIM = 128
Q_LORA_RANK = 384
KV_LORA_RANK = 256
ROPE_PAD = 128
QK_DIM = KV_LORA_RANK + ROPE_PAD
D_FF = 2816
ROPE_THETA = 10000.0
RMS_EPS = 1e-6
N_MOD = 6
ATTN_SCALE = (QK_NOPE_DIM + QK_ROPE_DIM) ** -0.5
PAGE_SIZE = 128
LANES = 128
NEG_BIG = float(jnp.finfo(jnp.float32).min)

V7X_VMEM_LIMIT_BYTES = 56 * 1024 * 1024
TOKEN_TILE = 512
HIDDEN_CHUNK = 256
ATTN_T = 256
ATTN_ROW_BLOCK = 64
DEC_KEY_CHUNK = 1024
DEC_HEAD_ROWS = 16


def _dot(a, b):
    return jnp.dot(a, b, preferred_element_type=F32)


def _dot_nt(a, b):
    return lax.dot_general(a, b, (((1,), (1,)), ((), ())), preferred_element_type=F32)


def _rms(x, g):
    return x * lax.rsqrt(jnp.mean(x * x, axis=-1, keepdims=True) + RMS_EPS) * g


def _silu(x):
    return x / (1.0 + jnp.exp(-x))


def _resident(shape):
    return pl.BlockSpec(shape, lambda *_: (0,) * len(shape), pipeline_mode=pl.Buffered(1))


def _mod_specs(seq_mode, tile):
    if seq_mode:
        return pl.BlockSpec((None, N_MOD, D_MODEL), lambda b, t: (b, 0, 0))
    return pl.BlockSpec((tile, N_MOD * D_MODEL), lambda b, t: (t, 0))


def _mod_get(mod_ref, k, seq_mode):
    if seq_mode:
        return mod_ref[k:k + 1, :]
    return mod_ref[:, k * D_MODEL:(k + 1) * D_MODEL]


def _mod_kernel(c_ref, w_ref, b_ref, o_ref):
    a = _silu(c_ref[...]).astype(BF16)
    o_ref[...] = _dot(a, w_ref[...].astype(BF16)) + b_ref[...]


def _modulation(c_all, mod_w, mod_b):
    depth, _, n_out = mod_w.shape
    rows = c_all.shape[0]
    tn = 1536
    return pl.pallas_call(
        _mod_kernel,
        out_shape=jax.ShapeDtypeStruct((depth, rows, n_out), F32),
        grid=(depth, n_out // tn),
        in_specs=[pl.BlockSpec((rows, D_MODEL), lambda l, n: (0, 0)),
                  pl.BlockSpec((None, D_MODEL, tn), lambda l, n: (l, 0, n)),
                  pl.BlockSpec((None, 1, tn), lambda l, n: (l, 0, n))],
        out_specs=pl.BlockSpec((None, rows, tn), lambda l, n: (l, 0, n)),
        compiler_params=pltpu.CompilerParams(dimension_semantics=("parallel", "parallel"),
                                             vmem_limit_bytes=V7X_VMEM_LIMIT_BYTES),
        name="adaln_modulation",
    )(c_all, mod_w, mod_b.reshape(depth, 1, n_out))


def _gmlp_kernel(*refs, kind, seq_mode, tile, hidden):
    if seq_mode:
        (x_ref, mod_ref, npre_ref, npost_ref, win_ref, cw_ref, cb_ref, wout_ref,
         xo_ref, st_ref, h_scr, acc_scr, halo_scr) = refs
    else:
        (x_ref, mod_ref, npre_ref, npost_ref, win_ref, cw_ref, cb_ref, wout_ref, p0_ref, p1_ref,
         xo_ref, st_ref, h_scr, acc_scr) = refs
    mo = 0 if kind == "mixer" else 3
    shift, scale, gate = (_mod_get(mod_ref, mo + k, seq_mode) for k in range(3))

    x = x_ref[...]
    h_scr[...] = (_rms(x, npre_ref[...]) * (1.0 + scale) + shift).astype(BF16)

    if seq_mode:
        @pl.when(pl.program_id(1) == 0)
        def _():
            halo_scr[...] = jnp.zeros_like(halo_scr)
        row = lax.broadcasted_iota(jnp.int32, (tile, 1), 0)

    ck = HIDDEN_CHUNK
    for c in range(hidden // ck):
        lo = c * ck
        hb = h_scr[...]
        if kind == "mixer":
            bg = _dot(hb, win_ref[:, lo:lo + ck])
            u = _dot(hb, win_ref[:, hidden + lo:hidden + lo + ck]) * _dot(hb, win_ref[:, 2 * hidden + lo:2 * hidden + lo + ck])
        else:
            u = _dot(hb, win_ref[:, lo:lo + ck])
            vg = _dot(hb, win_ref[:, hidden + lo:hidden + lo + ck])
        if seq_mode:
            h0 = halo_scr[0:1, lo:lo + ck]
            h1 = halo_scr[1:2, lo:lo + ck]
            u1 = jnp.where(row == 0, h1, pltpu.roll(u, 1, 0))
            u2 = jnp.where(row == 0, h0, jnp.where(row == 1, h1, pltpu.roll(u, 2, 0)))
            last2 = u[tile - 2:tile, :]
            halo_scr[:, lo:lo + ck] = last2
            st_ref[:, lo:lo + ck] = last2
        else:
            u1 = p1_ref[:, lo:lo + ck]
            u2 = p0_ref[:, lo:lo + ck]
            st_ref[:, lo:lo + ck] = u
        y = u2 * cw_ref[0:1, lo:lo + ck] + u1 * cw_ref[1:2, lo:lo + ck] + u * cw_ref[2:3, lo:lo + ck]
        if kind == "mixer":
            g = bg * y
        else:
            g = _silu(y + cb_ref[:, lo:lo + ck]) * vg
        part = _dot(g.astype(BF16), wout_ref[lo:lo + ck, :])
        if c == 0:
            acc_scr[...] = part
        else:
            acc_scr[...] += part

    xo_ref[...] = x_ref[...] + gate * _rms(acc_scr[...], npost_ref[...])


def _gmlp(x, mod, npre, npost, w_in, conv_w, conv_b, w_out, prev, *, kind, seq_mode):
    nb, length, _ = x.shape
    hidden = w_out.shape[0]
    tile = TOKEN_TILE if seq_mode else length
    grid = (nb, length // tile)
    x_spec = pl.BlockSpec((None, tile, D_MODEL), lambda b, t: (b, t, 0))
    in_specs = [x_spec, _mod_specs(seq_mode, tile),
                _resident((1, D_MODEL)), _resident((1, D_MODEL)),
                _resident(w_in.shape), _resident((3, hidden)), _resident((1, hidden)), _resident(w_out.shape)]
    args = [x, mod, npre.reshape(1, -1), npost.reshape(1, -1), w_in, conv_w, conv_b.reshape(1, -1), w_out]
    scratch = [pltpu.VMEM((tile, D_MODEL), BF16), pltpu.VMEM((tile, D_MODEL), F32)]
    if seq_mode:
        st_shape = jax.ShapeDtypeStruct((nb, 2, hidden), F32)
        st_spec = pl.BlockSpec((None, 2, hidden), lambda b, t: (b, 0, 0))
        scratch.append(pltpu.VMEM((2, hidden), F32))
    else:
        prev_spec = pl.BlockSpec((tile, hidden), lambda b, t: (t, 0))
        in_specs += [prev_spec, prev_spec]
        args += [prev[:, 0], prev[:, 1]]
        st_shape = jax.ShapeDtypeStruct((length, hidden), F32)
        st_spec = pl.BlockSpec((tile, hidden), lambda b, t: (t, 0))
    return pl.pallas_call(
        functools.partial(_gmlp_kernel, kind=kind, seq_mode=seq_mode, tile=tile, hidden=hidden),
        out_shape=(jax.ShapeDtypeStruct(x.shape, F32), st_shape),
        grid=grid,
        in_specs=in_specs,
        out_specs=(x_spec, st_spec),
        scratch_shapes=scratch,
        compiler_params=pltpu.CompilerParams(dimension_semantics=("parallel", "arbitrary"),
                                             vmem_limit_bytes=V7X_VMEM_LIMIT_BYTES),
        name=f"gated_conv_mlp_{kind}_{'seq' if seq_mode else 'tok'}",
    )(*args)


def _mla_proj_kernel(x_ref, mod_ref, npre_ref, wqa_ref, qan_ref, wqb_ref, wkva_ref, kvn_ref, wkt_ref,
                     cos_ref, sin_ref, q_out, kc_out, lat_out, pe_out, *, seq_mode):
    shift, scale = (_mod_get(mod_ref, k, seq_mode) for k in range(2))
    h = (_rms(x_ref[...], npre_ref[...]) * (1.0 + scale) + shift).astype(BF16)
    cos = cos_ref[...]
    sin = sin_ref[...]

    kv = _dot(h, wkva_ref[...])
    lat = _rms(kv[:, :KV_LORA_RANK], kvn_ref[...])
    pe = kv[:, KV_LORA_RANK:KV_LORA_RANK + ROPE_PAD] * cos + kv[:, KV_LORA_RANK + ROPE_PAD:] * sin
    lat_out[...] = lat
    pe_out[...] = pe[:, :QK_ROPE_DIM]
    kc_out[:, :KV_LORA_RANK] = lat.astype(BF16)
    kc_out[:, KV_LORA_RANK:] = pe.astype(BF16)

    qa = _rms(_dot(h, wqa_ref[...]), qan_ref[...]).astype(BF16)
    q = _dot(qa, wqb_ref[...])
    nope_w = N_HEADS * QK_NOPE_DIM
    pe_w = N_HEADS * ROPE_PAD
    for hd in range(N_HEADS):
        qn = q[:, hd * QK_NOPE_DIM:(hd + 1) * QK_NOPE_DIM].astype(BF16)
        ql = _dot(qn, wkt_ref[hd]) * ATTN_SCALE
        qp = (q[:, nope_w + hd * ROPE_PAD:nope_w + (hd + 1) * ROPE_PAD] * cos
              + q[:, nope_w + pe_w + hd * ROPE_PAD:nope_w + pe_w + (hd + 1) * ROPE_PAD] * sin) * ATTN_SCALE
        q_out[hd, :, :KV_LORA_RANK] = ql.astype(BF16)
        q_out[hd, :, KV_LORA_RANK:] = qp.astype(BF16)


def _mla_proj(x, mod, npre, w_qa, qa_norm, w_qb, w_kva, kv_norm, w_kt, cos, sin, *, seq_mode):
    nb, length, _ = x.shape
    tile = TOKEN_TILE if seq_mode else length
    row_spec = lambda w: pl.BlockSpec((None, tile, w), lambda b, t: (b, t, 0))
    tab_spec = pl.BlockSpec((tile, ROPE_PAD), lambda b, t: (t, 0))
    return pl.pallas_call(
        functools.partial(_mla_proj_kernel, seq_mode=seq_mode),
        out_shape=(jax.ShapeDtypeStruct((nb, N_HEADS, length, QK_DIM), BF16),
                   jax.ShapeDtypeStruct((nb, length, QK_DIM), BF16),
                   jax.ShapeDtypeStruct((nb, length, KV_LORA_RANK), F32),
                   jax.ShapeDtypeStruct((nb, length, QK_ROPE_DIM), F32)),
        grid=(nb, length // tile),
        in_specs=[row_spec(D_MODEL), _mod_specs(seq_mode, tile), _resident((1, D_MODEL)),
                  _resident(w_qa.shape), _resident((1, Q_LORA_RANK)), _resident(w_qb.shape),
                  _resident(w_kva.shape), _resident((1, KV_LORA_RANK)), _resident(w_kt.shape),
                  tab_spec, tab_spec],
        out_specs=(pl.BlockSpec((None, N_HEADS, tile, QK_DIM), lambda b, t: (b, 0, t, 0)),
                   row_spec(QK_DIM), row_spec(KV_LORA_RANK), row_spec(QK_ROPE_DIM)),
        compiler_params=pltpu.CompilerParams(dimension_semantics=("parallel", "parallel"),
                                             vmem_limit_bytes=V7X_VMEM_LIMIT_BYTES),
        name=f"mla_proj_{'seq' if seq_mode else 'tok'}",
    )(x, mod, npre.reshape(1, -1), w_qa, qa_norm.reshape(1, -1), w_qb, w_kva, kv_norm.reshape(1, -1), w_kt,
      cos, sin)


def _attn_kernel(q_ref, k_ref, o_ref, sa_scr, sb_scr, m_scr, l_scr, alpha_scr, p_scr, acc_scr):
    qi = pl.program_id(1)
    rows = N_HEADS * ATTN_T

    def key_rows(j):
        return pl.ds(pl.multiple_of(j * ATTN_T, ATTN_T), ATTN_T)

    def scores(j, s_scr):
        s_scr[...] = _dot_nt(q_ref[...].reshape(rows, QK_DIM), k_ref[key_rows(j), :])

    def update(j, s_scr, masked):
        v = k_ref[key_rows(j), :KV_LORA_RANK]
        for hd in range(N_HEADS):
            for tok0 in range(0, ATTN_T, ATTN_ROW_BLOCK):
                r = slice(hd * ATTN_T + tok0, hd * ATTN_T + tok0 + ATTN_ROW_BLOCK)
                s = s_scr[r, :]
                if masked:
                    tok = tok0 + lax.broadcasted_iota(jnp.int32, (ATTN_ROW_BLOCK, 1), 0)
                    key = lax.broadcasted_iota(jnp.int32, (1, ATTN_T), 1)
                    s = jnp.where(key <= tok, s, NEG_BIG)
                m_prev = m_scr[r, :]
                m_new = jnp.maximum(m_prev, jnp.max(s, axis=-1, keepdims=True))
                alpha = jnp.exp(m_prev - m_new)
                p = jnp.exp(s - jnp.tile(m_new, (1, ATTN_T // LANES)))
                l_scr[r, :] = alpha * l_scr[r, :] + jnp.sum(p, axis=-1, keepdims=True)
                m_scr[r, :] = m_new
                alpha_scr[r, :] = alpha
                p_scr[r, :] = p.astype(BF16)
            r = slice(hd * ATTN_T, (hd + 1) * ATTN_T)
            acc_scr[r, :] = (jnp.tile(alpha_scr[r, :], (1, KV_LORA_RANK // LANES)) * acc_scr[r, :]
                             + _dot(p_scr[r, :], v))

    m_scr[...] = jnp.full_like(m_scr, NEG_BIG)
    l_scr[...] = jnp.zeros_like(l_scr)
    acc_scr[...] = jnp.zeros_like(acc_scr)
    scores(0, sa_scr)

    def pair(t, carry):
        scores(2 * t + 1, sb_scr)
        update(2 * t, sa_scr, False)
        scores(2 * t + 2, sa_scr)
        update(2 * t + 1, sb_scr, False)
        return carry
    lax.fori_loop(0, qi // 2, pair, 0)

    @pl.when(qi % 2 == 0)
    def _():
        update(qi, sa_scr, True)

    @pl.when(qi % 2 == 1)
    def _():
        scores(qi, sb_scr)
        update(qi - 1, sa_scr, False)
        update(qi, sb_scr, True)

    for hd in range(N_HEADS):
        r = slice(hd * ATTN_T, (hd + 1) * ATTN_T)
        o_ref[hd] = (acc_scr[r, :] / jnp.tile(l_scr[r, :], (1, KV_LORA_RANK // LANES))).astype(BF16)


def _attention(q, kc):
    nb, _, length, _ = q.shape
    rows = N_HEADS * ATTN_T
    return pl.pallas_call(
        _attn_kernel,
        out_shape=jax.ShapeDtypeStruct((nb, N_HEADS, length, KV_LORA_RANK), BF16),
        grid=(nb, length // ATTN_T),
        in_specs=[pl.BlockSpec((None, N_HEADS, ATTN_T, QK_DIM), lambda b, i: (b, 0, i, 0)),
                  pl.BlockSpec((None, length, QK_DIM), lambda b, i: (b, 0, 0))],
        out_specs=pl.BlockSpec((None, N_HEADS, ATTN_T, KV_LORA_RANK), lambda b, i: (b, 0, i, 0)),
        scratch_shapes=[pltpu.VMEM((rows, ATTN_T), F32), pltpu.VMEM((rows, ATTN_T), F32),
                        pltpu.VMEM((rows, LANES), F32), pltpu.VMEM((rows, LANES), F32),
                        pltpu.VMEM((rows, LANES), F32), pltpu.VMEM((rows, ATTN_T), BF16),
                        pltpu.VMEM((rows, KV_LORA_RANK), F32)],
        compiler_params=pltpu.CompilerParams(dimension_semantics=("parallel", "arbitrary"),
                                             vmem_limit_bytes=V7X_VMEM_LIMIT_BYTES),
        name="mla_causal_attention",
    )(q, kc)


def _dec_attn_kernel(pt_ref, q_ref, kn_ref, lat_hbm, pe_hbm, o_ref, lat0, lat1, pe0, pe1, kb_scr, sems,
                     *, layer, n_pages):
    g = pl.program_id(0)
    bufs = ((lat0, pe0), (lat1, pe1))
    past = n_pages * PAGE_SIZE

    def page_copies(seq, half, page):
        src = pt_ref[seq * n_pages + page]
        dst = pl.ds(page * PAGE_SIZE, PAGE_SIZE)
        lat_buf, pe_buf = bufs[half]
        return (pltpu.make_async_copy(lat_hbm.at[layer, src], lat_buf.at[dst, :], sems.at[0, half]),
                pltpu.make_async_copy(pe_hbm.at[layer, src], pe_buf.at[dst, :], sems.at[1, half]))

    def start_pages(seq, half, pages):
        for page in pages:
            for cp in page_copies(seq, half, page):
                cp.start()

    def wait_seq(seq, half):
        for page in range(n_pages):
            for cp in page_copies(seq, half, page):
                cp.wait()

    chunks = [slice(c, c + DEC_KEY_CHUNK) for c in range(0, past, DEC_KEY_CHUNK)]
    issue_slices = max(1, len(chunks) // 2)
    pages_per_slice = -(-n_pages // issue_slices)

    def attend(half, next_seq):
        lat_buf, pe_buf = bufs[half]
        q = q_ref[half]
        kn = kn_ref[half].astype(F32)
        ql = q[:, :KV_LORA_RANK]
        qp = q[:, KV_LORA_RANK:KV_LORA_RANK + QK_ROPE_DIM]
        parts = []
        for ci, rows in enumerate(chunks):
            if next_seq is not None:
                start_pages(next_seq, 1 - half,
                            range(min(n_pages, ci * pages_per_slice), min(n_pages, (ci + 1) * pages_per_slice)))
            lat = lat_buf[rows, :].astype(BF16)
            kb_scr[rows, :] = lat
            parts.append(_dot_nt(ql, lat) + _dot_nt(qp, pe_buf[rows, :].astype(BF16)))
        s = jnp.concatenate(parts, axis=1)
        s_new = jnp.sum(q.astype(F32) * kn, axis=-1, keepdims=True)
        m = jnp.maximum(jnp.max(s, axis=-1, keepdims=True), s_new)
        p = jnp.exp(s - m)
        p_new = jnp.exp(s_new - m)
        denom = jnp.sum(p, axis=-1, keepdims=True) + p_new
        pb = p.astype(BF16)
        o = [p_new * kn[:, :KV_LORA_RANK], jnp.zeros((DEC_HEAD_ROWS, KV_LORA_RANK), F32)]
        for ci, rows in enumerate(chunks):
            o[ci % 2] = o[ci % 2] + _dot(pb[:, rows], kb_scr[rows, :])
        o_ref[half] = ((o[0] + o[1]) / denom).astype(BF16)

    @pl.when(g == 0)
    def _():
        start_pages(0, 0, range(n_pages))

    wait_seq(2 * g, 0)
    attend(0, 2 * g + 1)
    wait_seq(2 * g + 1, 1)
    is_last = g + 1 == pl.num_programs(0)

    @pl.when(jnp.logical_not(is_last))
    def _():
        attend(1, 2 * g + 2)

    @pl.when(is_last)
    def _():
        attend(1, None)


def _decode_attention(q, kn, cache_lat, cache_pe, page_table, layer):
    n_seq, n_pages = page_table.shape
    past = n_pages * PAGE_SIZE
    assert n_seq % 2 == 0 and past % DEC_KEY_CHUNK == 0
    grid_spec = pltpu.PrefetchScalarGridSpec(
        num_scalar_prefetch=1,
        grid=(n_seq // 2,),
        in_specs=[pl.BlockSpec((2, DEC_HEAD_ROWS, QK_DIM), lambda g, pt: (g, 0, 0)),
                  pl.BlockSpec((2, 1, QK_DIM), lambda g, pt: (g, 0, 0)),
                  pl.BlockSpec(memory_space=pl.ANY),
                  pl.BlockSpec(memory_space=pl.ANY)],
        out_specs=pl.BlockSpec((2, DEC_HEAD_ROWS, KV_LORA_RANK), lambda g, pt: (g, 0, 0)),
        scratch_shapes=[pltpu.VMEM((past, KV_LORA_RANK), F32), pltpu.VMEM((past, KV_LORA_RANK), F32),
                        pltpu.VMEM((past, QK_ROPE_DIM), F32), pltpu.VMEM((past, QK_ROPE_DIM), F32),
                        pltpu.VMEM((past, KV_LORA_RANK), BF16),
                        pltpu.SemaphoreType.DMA((2, 2))],
    )
    return pl.pallas_call(
        functools.partial(_dec_attn_kernel, layer=layer, n_pages=n_pages),
        out_shape=jax.ShapeDtypeStruct((n_seq, DEC_HEAD_ROWS, KV_LORA_RANK), BF16),
        grid_spec=grid_spec,
        compiler_params=pltpu.CompilerParams(dimension_semantics=("arbitrary",),
                                             vmem_limit_bytes=V7X_VMEM_LIMIT_BYTES),
        name="mla_paged_decode_attention",
    )(page_table.reshape(-1), q, kn, cache_lat, cache_pe)


def _mla_out_kernel(ol_ref, x_ref, mod_ref, npost_ref, wv_ref, wo_ref, xo_ref, ov_scr, *, seq_mode):
    gate = _mod_get(mod_ref, 2, seq_mode)
    for hd in range(N_HEADS):
        ov_scr[:, hd * V_HEAD_DIM:(hd + 1) * V_HEAD_DIM] = _dot(ol_ref[hd], wv_ref[hd]).astype(BF16)
    o = _dot(ov_scr[...], wo_ref[...])
    xo_ref[...] = x_ref[...] + gate * _rms(o, npost_ref[...])


def _mla_out(o_lat, x, mod, npost, w_v, w_o, *, seq_mode):
    nb, length, _ = x.shape
    tile = TOKEN_TILE if seq_mode else length
    x_spec = pl.BlockSpec((None, tile, D_MODEL), lambda b, t: (b, t, 0))
    return pl.pallas_call(
        functools.partial(_mla_out_kernel, seq_mode=seq_mode),
        out_shape=jax.ShapeDtypeStruct(x.shape, F32),
        grid=(nb, length // tile),
        in_specs=[pl.BlockSpec((None, N_HEADS, tile, KV_LORA_RANK), lambda b, t: (b, 0, t, 0)),
                  x_spec, _mod_specs(seq_mode, tile), _resident((1, D_MODEL)),
                  _resident(w_v.shape), _resident(w_o.shape)],
        out_specs=x_spec,
        scratch_shapes=[pltpu.VMEM((tile, N_HEADS * V_HEAD_DIM), BF16)],
        compiler_params=pltpu.CompilerParams(dimension_semantics=("parallel", "parallel"),
                                             vmem_limit_bytes=V7X_VMEM_LIMIT_BYTES),
        name=f"mla_out_{'seq' if seq_mode else 'tok'}",
    )(o_lat, x, mod, npost.reshape(1, -1), w_v, w_o)


def _pad_lanes(w, width):
    return jnp.pad(w, [(0, 0)] * (w.ndim - 1) + [(0, width - w.shape[-1])])


def _swap_halves(w):
    return jnp.concatenate([w[..., ROPE_HALF:], w[..., :ROPE_HALF]], axis=-1)


def _prep_mla_weights(w_qb, w_kva, w_kvb):
    qb = w_qb.reshape(Q_LORA_RANK, N_HEADS, QK_NOPE_DIM + QK_ROPE_DIM)
    q_nope = qb[..., :QK_NOPE_DIM].reshape(Q_LORA_RANK, -1)
    q_pe = qb[..., QK_NOPE_DIM:]
    w_qb2 = jnp.concatenate([q_nope,
                             _pad_lanes(q_pe, ROPE_PAD).reshape(Q_LORA_RANK, -1),
                             _pad_lanes(_swap_halves(q_pe), ROPE_PAD).reshape(Q_LORA_RANK, -1)], axis=-1)
    k_pe = w_kva[:, KV_LORA_RANK:]
    w_kva2 = jnp.concatenate([w_kva[:, :KV_LORA_RANK], _pad_lanes(k_pe, ROPE_PAD),
                              _pad_lanes(_swap_halves(k_pe), ROPE_PAD)], axis=-1)
    kvb = w_kvb.reshape(KV_LORA_RANK, N_HEADS, QK_NOPE_DIM + V_HEAD_DIM)
    w_kt = jnp.transpose(kvb[..., :QK_NOPE_DIM], (1, 2, 0))
    w_v = jnp.transpose(kvb[..., QK_NOPE_DIM:], (1, 0, 2))
    return w_qb2.astype(BF16), w_kva2.astype(BF16), w_kt.astype(BF16), w_v.astype(BF16)


def _rope_tables(pos):
    inv = ROPE_THETA ** (-jnp.arange(ROPE_HALF, dtype=F32) * 2.0 / QK_ROPE_DIM)
    ang = pos.astype(F32)[:, None] * inv[None, :]
    cos, sin = jnp.cos(ang), jnp.sin(ang)
    return (_pad_lanes(jnp.concatenate([cos, cos], axis=-1), ROPE_PAD),
            _pad_lanes(jnp.concatenate([-sin, sin], axis=-1), ROPE_PAD))


def _trunk(x, mods, pos, mix_prev, ffn_prev, attn_past, w, *, seq_mode):
    depth = len(mods)
    cos, sin = _rope_tables(pos)
    new_lat, new_pe, new_mix, new_ffn = [], [], [], []
    for i in range(depth):
        j = i // 2
        mod = mods[i]
        if i % 2 == 0:
            x, st = _gmlp(x, mod, w["norm_mix_pre"][i], w["norm_mix_post"][i], w["sc_w_in"][j], w["sc_conv_w"][j],
                          jnp.zeros((D_MODEL,), F32), w["sc_w_out"][j],
                          None if seq_mode else mix_prev[j], kind="mixer", seq_mode=seq_mode)
            new_mix.append(st if seq_mode else jnp.stack([mix_prev[j][:, 1], st], axis=1))
        else:
            q, kc, lat, pe = _mla_proj(x, mod, w["norm_mix_pre"][i], w["mla_w_qa"][j], w["mla_qa_norm"][j],
                                       w["mla_w_qb"][j], w["mla_w_kva"][j], w["mla_kv_norm"][j], w["mla_w_kt"][j],
                                       cos, sin, seq_mode=seq_mode)
            if seq_mode:
                o_lat = _attention(q, kc)
            else:
                cache_lat, cache_pe, page_table = attn_past
                qs = jnp.pad(jnp.transpose(q[0], (1, 0, 2)), ((0, 0), (0, DEC_HEAD_ROWS - N_HEADS), (0, 0)))
                o = _decode_attention(qs, jnp.transpose(kc, (1, 0, 2)), cache_lat, cache_pe, page_table, j)
                o_lat = jnp.transpose(o[:, :N_HEADS], (1, 0, 2))[None]
            x = _mla_out(o_lat, x, mod, w["norm_mix_post"][i], w["mla_w_v"][j], w["mla_w_o"][j], seq_mode=seq_mode)
            new_lat.append(lat)
            new_pe.append(pe)
        x, st = _gmlp(x, mod, w["norm_ffn_pre"][i], w["norm_ffn_post"][i], w["ffn_w_in"][i], w["ffn_conv_w"][i],
                      w["ffn_conv_b"][i], w["ffn_w_out"][i],
                      None if seq_mode else ffn_prev[i], kind="ffn", seq_mode=seq_mode)
        new_ffn.append(st if seq_mode else jnp.stack([ffn_prev[i][:, 1], st], axis=1))
    return x, jnp.stack(new_lat), jnp.stack(new_pe), jnp.stack(new_mix), jnp.stack(new_ffn)


def kernel(x_prompt, x_sample, cache_kv_latent, cache_k_rope, state_mixconv, state_ffnconv, page_table, c_prompt, c_sample, mod_w, mod_b, norm_mix_pre, norm_mix_post, norm_ffn_pre, norm_ffn_post, sc_w_in, sc_conv_w, sc_w_out, mla_w_qa, mla_qa_norm, mla_w_qb, mla_w_kva, mla_kv_norm, mla_w_kvb, mla_w_o, ffn_w_in, ffn_conv_w, ffn_conv_b, ffn_w_out):
    depth = mod_w.shape[0]
    n_attn = mla_w_qb.shape[0]
    n_prompt, seq_len, _ = x_prompt.shape
    n_sample, dec_len, _ = x_sample.shape
    assert dec_len == 1 and seq_len % TOKEN_TILE == 0 and seq_len % ATTN_T == 0

    prepped = [_prep_mla_weights(mla_w_qb[j], mla_w_kva[j], mla_w_kvb[j]) for j in range(n_attn)]
    w = {
        "norm_mix_pre": norm_mix_pre, "norm_mix_post": norm_mix_post,
        "norm_ffn_pre": norm_ffn_pre, "norm_ffn_post": norm_ffn_post,
        "sc_w_in": sc_w_in.astype(BF16), "sc_conv_w": sc_conv_w, "sc_w_out": sc_w_out.astype(BF16),
        "mla_w_qa": mla_w_qa.astype(BF16), "mla_qa_norm": mla_qa_norm, "mla_kv_norm": mla_kv_norm,
        "mla_w_qb": [p[0] for p in prepped], "mla_w_kva": [p[1] for p in prepped],
        "mla_w_kt": [p[2] for p in prepped], "mla_w_v": [p[3] for p in prepped],
        "mla_w_o": mla_w_o.astype(BF16),
        "ffn_w_in": ffn_w_in.astype(BF16), "ffn_conv_w": ffn_conv_w, "ffn_conv_b": ffn_conv_b,
        "ffn_w_out": ffn_w_out.astype(BF16),
    }

    mod_all = _modulation(jnp.concatenate([c_sample, c_prompt], axis=0), mod_w, mod_b)
    mods_s = [mod_all[i, :n_sample] for i in range(depth)]
    mods_p = [mod_all[i, n_sample:].reshape(n_prompt, N_MOD, D_MODEL) for i in range(depth)]

    pos_p = jnp.arange(seq_len, dtype=jnp.int32)
    y_p, lat_p, pe_p, mix_p, ffn_p = _trunk(x_prompt, mods_p, pos_p, None, None, None, w, seq_mode=True)

    past_len = page_table.shape[1] * cache_kv_latent.shape[2]
    pos_s = jnp.full((n_sample,), past_len, dtype=jnp.int32)
    xs = x_sample.reshape(1, n_sample, D_MODEL)
    y_s, lat_s, pe_s, mix_s, ffn_s = _trunk(xs, mods_s, pos_s, state_mixconv, state_ffnconv,
                                            (cache_kv_latent, cache_k_rope, page_table), w, seq_mode=False)

    return (y_p, y_s.reshape(n_sample, 1, D_MODEL), lat_p, pe_p,
            lat_s.reshape(n_attn, n_sample, 1, KV_LORA_RANK), pe_s.reshape(n_attn, n_sample, 1, QK_ROPE_DIM),
            mix_p, mix_s, ffn_p, ffn_s)
```

```python
import functools

import jax
import jax.numpy as jnp
from jax import lax
from jax.experimental import pallas as pl
from jax.experimental.pallas import tpu as pltpu

F32 = jnp.float32
BF16 = jnp.bfloat16

D_MODEL = 1024
N_HEADS = 8
QK_NOPE_DIM = 128
QK_ROPE_DIM = 64
ROPE_HALF = QK_ROPE_DIM // 2
V_HEAD_DIM = 128
Q_LORA_RANK = 384
KV_LORA_RANK = 256
ROPE_PAD = 128
QK_DIM = KV_LORA_RANK + ROPE_PAD
D_FF = 2816
ROPE_THETA = 10000.0
RMS_EPS = 1e-6
N_MOD = 6
ATTN_SCALE = (QK_NOPE_DIM + QK_ROPE_DIM) ** -0.5
PAGE_SIZE = 128
LANES = 128
SUBLANES = 8
NEG_BIG = float(jnp.finfo(jnp.float32).min)

V7X_VMEM_LIMIT_BYTES = 56 * 1024 * 1024
TOKEN_TILE = 1024
HIDDEN_CHUNK = 256
ATTN_T = 256
ATTN_ROW_BLOCK = 64
DEC_KEY_CHUNK = 1024
DEC_HEAD_ROWS = 16


def _dot(a, b):
    return jnp.dot(a, b, preferred_element_type=F32)


def _dot_nt(a, b):
    return lax.dot_general(a, b, (((1,), (1,)), ((), ())), preferred_element_type=F32)


def _rms(x, g):
    return x * lax.rsqrt(jnp.mean(x * x, axis=-1, keepdims=True) + RMS_EPS) * g


def _silu(x):
    return x / (1.0 + jnp.exp(-x))


def _resident(shape):
    return pl.BlockSpec(shape, lambda *_: (0,) * len(shape), pipeline_mode=pl.Buffered(1))


def _mod_specs(seq_mode, tile):
    if seq_mode:
        return pl.BlockSpec((None, N_MOD, D_MODEL), lambda b, t: (b, 0, 0))
    return pl.BlockSpec((tile, N_MOD * D_MODEL), lambda b, t: (t, 0))


def _mod_get(mod_ref, k, seq_mode):
    if seq_mode:
        return mod_ref[k:k + 1, :]
    return mod_ref[:, k * D_MODEL:(k + 1) * D_MODEL]


def _mod_kernel(c_ref, w_ref, b_ref, o_ref):
    a = _silu(c_ref[...]).astype(BF16)
    o_ref[...] = _dot(a, w_ref[...].astype(BF16)) + b_ref[...]


def _modulation(c_all, mod_w, mod_b):
    depth, _, n_out = mod_w.shape
    rows = c_all.shape[0]
    tn = 1536
    return pl.pallas_call(
        _mod_kernel,
        out_shape=jax.ShapeDtypeStruct((depth, rows, n_out), F32),
        grid=(depth, n_out // tn),
        in_specs=[pl.BlockSpec((rows, D_MODEL), lambda l, n: (0, 0)),
                  pl.BlockSpec((None, D_MODEL, tn), lambda l, n: (l, 0, n)),
                  pl.BlockSpec((None, 1, tn), lambda l, n: (l, 0, n))],
        out_specs=pl.BlockSpec((None, rows, tn), lambda l, n: (l, 0, n)),
        compiler_params=pltpu.CompilerParams(dimension_semantics=("parallel", "parallel"),
                                             vmem_limit_bytes=V7X_VMEM_LIMIT_BYTES),
        name="adaln_modulation",
    )(c_all, mod_w, mod_b.reshape(depth, 1, n_out))


def _gmlp_kernel(*refs, kind, seq_mode, tile, hidden):
    if seq_mode:
        (x_ref, mod_ref, npre_ref, npost_ref, win_ref, cw_ref, cb_ref, wout_ref,
         xo_ref, st_ref, h_scr, g_scr, pre_scr, halo_scr) = refs
    else:
        (x_ref, mod_ref, npre_ref, npost_ref, win_ref, cw_ref, cb_ref, wout_ref, p0_ref, p1_ref,
         xo_ref, st_ref, h_scr, g_scr, pre_scr) = refs
    mo = 0 if kind == "mixer" else 3
    shift, scale, gate = (_mod_get(mod_ref, mo + k, seq_mode) for k in range(3))

    x = x_ref[...]
    h_scr[...] = (_rms(x, npre_ref[...]) * (1.0 + scale) + shift).astype(BF16)

    if seq_mode:
        @pl.when(pl.program_id(1) == 0)
        def _():
            halo_scr[...] = jnp.zeros_like(halo_scr)
        top_row = lax.broadcasted_iota(jnp.int32, (SUBLANES, 1), 0)

    ck = HIDDEN_CHUNK
    n_gates = 3 if kind == "mixer" else 2

    def project_in(c, slot):
        hb = h_scr[...]
        for gi in range(n_gates):
            lo = gi * hidden + c * ck
            pre_scr[slot, gi] = _dot(hb, win_ref[:, lo:lo + ck])

    project_in(0, 0)
    for c in range(hidden // ck):
        lo = c * ck
        slot = c % 2
        if c + 1 < hidden // ck:
            project_in(c + 1, 1 - slot)
        if kind == "mixer":
            bg = pre_scr[slot, 0]
            u = pre_scr[slot, 1] * pre_scr[slot, 2]
        else:
            u = pre_scr[slot, 0]
            vg = pre_scr[slot, 1]
        w0, w1, w2 = (cw_ref[k:k + 1, lo:lo + ck] for k in range(3))
        if seq_mode:
            u1 = pltpu.roll(u, 1, 0)
            u2 = pltpu.roll(u, 2, 0)
            y = u2 * w0 + u1 * w1 + u * w2
            h0 = halo_scr[0:1, lo:lo + ck]
            h1 = halo_scr[1:2, lo:lo + ck]
            u1_top = jnp.where(top_row == 0, h1, u1[:SUBLANES])
            u2_top = jnp.where(top_row == 0, h0, jnp.where(top_row == 1, h1, u2[:SUBLANES]))
            y_top = u2_top * w0 + u1_top * w1 + u[:SUBLANES] * w2
            y = jnp.concatenate([y_top, y[SUBLANES:]], axis=0)
            last2 = u[tile - 2:tile, :]
            halo_scr[:, lo:lo + ck] = last2
            st_ref[:, lo:lo + ck] = last2
        else:
            y = p0_ref[:, lo:lo + ck] * w0 + p1_ref[:, lo:lo + ck] * w1 + u * w2
            st_ref[:, lo:lo + ck] = u
        if kind == "mixer":
            g = bg * y
        else:
            g = _silu(y + cb_ref[:, lo:lo + ck]) * vg
        g_scr[:, lo:lo + ck] = g.astype(BF16)

    o = _dot(g_scr[...], wout_ref[...])
    xo_ref[...] = x_ref[...] + gate * _rms(o, npost_ref[...])


def _gmlp(x, mod, npre, npost, w_in, conv_w, conv_b, w_out, prev, *, kind, seq_mode):
    nb, length, _ = x.shape
    hidden = w_out.shape[0]
    tile = TOKEN_TILE if seq_mode else length
    grid = (nb, length // tile)
    x_spec = pl.BlockSpec((None, tile, D_MODEL), lambda b, t: (b, t, 0))
    in_specs = [x_spec, _mod_specs(seq_mode, tile),
                _resident((1, D_MODEL)), _resident((1, D_MODEL)),
                _resident(w_in.shape), _resident((3, hidden)), _resident((1, hidden)), _resident(w_out.shape)]
    args = [x, mod, npre.reshape(1, -1), npost.reshape(1, -1), w_in, conv_w, conv_b.reshape(1, -1), w_out]
    n_gates = 3 if kind == "mixer" else 2
    scratch = [pltpu.VMEM((tile, D_MODEL), BF16), pltpu.VMEM((tile, hidden), BF16),
               pltpu.VMEM((2, n_gates, tile, HIDDEN_CHUNK), F32)]
    if seq_mode:
        st_shape = jax.ShapeDtypeStruct((nb, 2, hidden), F32)
        st_spec = pl.BlockSpec((None, 2, hidden), lambda b, t: (b, 0, 0))
        scratch.append(pltpu.VMEM((2, hidden), F32))
    else:
        prev_spec = pl.BlockSpec((tile, hidden), lambda b, t: (t, 0))
        in_specs += [prev_spec, prev_spec]
        args += [prev[:, 0], prev[:, 1]]
        st_shape = jax.ShapeDtypeStruct((length, hidden), F32)
        st_spec = pl.BlockSpec((tile, hidden), lambda b, t: (t, 0))
    return pl.pallas_call(
        functools.partial(_gmlp_kernel, kind=kind, seq_mode=seq_mode, tile=tile, hidden=hidden),
        out_shape=(jax.ShapeDtypeStruct(x.shape, F32), st_shape),
        grid=grid,
        in_specs=in_specs,
        out_specs=(x_spec, st_spec),
        scratch_shapes=scratch,
        compiler_params=pltpu.CompilerParams(dimension_semantics=("parallel", "arbitrary"),
                                             vmem_limit_bytes=V7X_VMEM_LIMIT_BYTES),
        name=f"gated_conv_mlp_{kind}_{'seq' if seq_mode else 'tok'}",
    )(*args)


def _mla_proj_kernel(x_ref, mod_ref, npre_ref, wqa_ref, qan_ref, wqb_ref, wkva_ref, kvn_ref, wkt_ref,
                     cos_ref, sin_ref, q_out, kc_out, lat_out, pe_out, *, seq_mode):
    shift, scale = (_mod_get(mod_ref, k, seq_mode) for k in range(2))
    h = (_rms(x_ref[...], npre_ref[...]) * (1.0 + scale) + shift).astype(BF16)
    cos = cos_ref[...]
    sin = sin_ref[...]

    kv = _dot(h, wkva_ref[...])
    lat = _rms(kv[:, :KV_LORA_RANK], kvn_ref[...])
    pe = kv[:, KV_LORA_RANK:KV_LORA_RANK + ROPE_PAD] * cos + kv[:, KV_LORA_RANK + ROPE_PAD:] * sin
    lat_out[...] = lat
    pe_out[...] = pe[:, :QK_ROPE_DIM]
    kc_out[:, :KV_LORA_RANK] = lat.astype(BF16)
    kc_out[:, KV_LORA_RANK:] = pe.astype(BF16)

    qa = _rms(_dot(h, wqa_ref[...]), qan_ref[...]).astype(BF16)
    q = _dot(qa, wqb_ref[...])
    nope_w = N_HEADS * QK_NOPE_DIM
    pe_w = N_HEADS * ROPE_PAD
    for hd in range(N_HEADS):
        qn = q[:, hd * QK_NOPE_DIM:(hd + 1) * QK_NOPE_DIM].astype(BF16)
        ql = _dot(qn, wkt_ref[hd]) * ATTN_SCALE
        qp = (q[:, nope_w + hd * ROPE_PAD:nope_w + (hd + 1) * ROPE_PAD] * cos
              + q[:, nope_w + pe_w + hd * ROPE_PAD:nope_w + pe_w + (hd + 1) * ROPE_PAD] * sin) * ATTN_SCALE
        q_out[hd, :, :KV_LORA_RANK] = ql.astype(BF16)
        q_out[hd, :, KV_LORA_RANK:] = qp.astype(BF16)


def _mla_proj(x, mod, npre, w_qa, qa_norm, w_qb, w_kva, kv_norm, w_kt, cos, sin, *, seq_mode):
    nb, length, _ = x.shape
    tile = TOKEN_TILE if seq_mode else length
    row_spec = lambda w: pl.BlockSpec((None, tile, w), lambda b, t: (b, t, 0))
    tab_spec = pl.BlockSpec((tile, ROPE_PAD), lambda b, t: (t, 0))
    return pl.pallas_call(
        functools.partial(_mla_proj_kernel, seq_mode=seq_mode),
        out_shape=(jax.ShapeDtypeStruct((nb, N_HEADS, length, QK_DIM), BF16),
                   jax.ShapeDtypeStruct((nb, length, QK_DIM), BF16),
                   jax.ShapeDtypeStruct((nb, length, KV_LORA_RANK), F32),
                   jax.ShapeDtypeStruct((nb, length, QK_ROPE_DIM), F32)),
        grid=(nb, length // tile),
        in_specs=[row_spec(D_MODEL), _mod_specs(seq_mode, tile), _resident((1, D_MODEL)),
                  _resident(w_qa.shape), _resident((1, Q_LORA_RANK)), _resident(w_qb.shape),
                  _resident(w_kva.shape), _resident((1, KV_LORA_RANK)), _resident(w_kt.shape),
                  tab_spec, tab_spec],
        out_specs=(pl.BlockSpec((None, N_HEADS, tile, QK_DIM), lambda b, t: (b, 0, t, 0)),
                   row_spec(QK_DIM), row_spec(KV_LORA_RANK), row_spec(QK_ROPE_DIM)),
        compiler_params=pltpu.CompilerParams(dimension_semantics=("parallel", "parallel"),
                                             vmem_limit_bytes=V7X_VMEM_LIMIT_BYTES),
        name=f"mla_proj_{'seq' if seq_mode else 'tok'}",
    )(x, mod, npre.reshape(1, -1), w_qa, qa_norm.reshape(1, -1), w_qb, w_kva, kv_norm.reshape(1, -1), w_kt,
      cos, sin)


def _attn_kernel(q_ref, k_ref, o_ref, sa_scr, sb_scr, m_scr, l_scr, alpha_scr, p_scr, acc_scr):
    qi = pl.program_id(1)
    rows = N_HEADS * ATTN_T

    def key_rows(j):
        return pl.ds(pl.multiple_of(j * ATTN_T, ATTN_T), ATTN_T)

    def scores(j, s_scr):
        s_scr[...] = _dot_nt(q_ref[...].reshape(rows, QK_DIM), k_ref[key_rows(j), :])

    def update(j, s_scr, masked):
        v = k_ref[key_rows(j), :KV_LORA_RANK]
        for hd in range(N_HEADS):
            for tok0 in range(0, ATTN_T, ATTN_ROW_BLOCK):
                r = slice(hd * ATTN_T + tok0, hd * ATTN_T + tok0 + ATTN_ROW_BLOCK)
                s = s_scr[r, :]
                if masked:
                    tok = tok0 + lax.broadcasted_iota(jnp.int32, (ATTN_ROW_BLOCK, 1), 0)
                    key = lax.broadcasted_iota(jnp.int32, (1, ATTN_T), 1)
                    s = jnp.where(key <= tok, s, NEG_BIG)
                m_prev = m_scr[r, :]
                m_new = jnp.maximum(m_prev, jnp.max(s, axis=-1, keepdims=True))
                alpha = jnp.exp(m_prev - m_new)
                p = jnp.exp(s - jnp.tile(m_new, (1, ATTN_T // LANES)))
                l_scr[r, :] = alpha * l_scr[r, :] + jnp.sum(p, axis=-1, keepdims=True)
                m_scr[r, :] = m_new
                alpha_scr[r, :] = alpha
                p_scr[r, :] = p.astype(BF16)
            r = slice(hd * ATTN_T, (hd + 1) * ATTN_T)
            acc_scr[r, :] = (jnp.tile(alpha_scr[r, :], (1, KV_LORA_RANK // LANES)) * acc_scr[r, :]
                             + _dot(p_scr[r, :], v))

    m_scr[...] = jnp.full_like(m_scr, NEG_BIG)
    l_scr[...] = jnp.zeros_like(l_scr)
    acc_scr[...] = jnp.zeros_like(acc_scr)
    scores(0, sa_scr)

    def pair(t, carry):
        scores(2 * t + 1, sb_scr)
        update(2 * t, sa_scr, False)
        scores(2 * t + 2, sa_scr)
        update(2 * t + 1, sb_scr, False)
        return carry
    lax.fori_loop(0, qi // 2, pair, 0)

    @pl.when(qi % 2 == 0)
    def _():
        update(qi, sa_scr, True)

    @pl.when(qi % 2 == 1)
    def _():
        scores(qi, sb_scr)
        update(qi - 1, sa_scr, False)
        update(qi, sb_scr, True)

    for hd in range(N_HEADS):
        r = slice(hd * ATTN_T, (hd + 1) * ATTN_T)
        o_ref[hd] = (acc_scr[r, :] / jnp.tile(l_scr[r, :], (1, KV_LORA_RANK // LANES))).astype(BF16)


def _attention(q, kc):
    nb, _, length, _ = q.shape
    rows = N_HEADS * ATTN_T
    return pl.pallas_call(
        _attn_kernel,
        out_shape=jax.ShapeDtypeStruct((nb, N_HEADS, length, KV_LORA_RANK), BF16),
        grid=(nb, length // ATTN_T),
        in_specs=[pl.BlockSpec((None, N_HEADS, ATTN_T, QK_DIM), lambda b, i: (b, 0, i, 0)),
                  pl.BlockSpec((None, length, QK_DIM), lambda b, i: (b, 0, 0))],
        out_specs=pl.BlockSpec((None, N_HEADS, ATTN_T, KV_LORA_RANK), lambda b, i: (b, 0, i, 0)),
        scratch_shapes=[pltpu.VMEM((rows, ATTN_T), F32), pltpu.VMEM((rows, ATTN_T), F32),
                        pltpu.VMEM((rows, LANES), F32), pltpu.VMEM((rows, LANES), F32),
                        pltpu.VMEM((rows, LANES), F32), pltpu.VMEM((rows, ATTN_T), BF16),
                        pltpu.VMEM((rows, KV_LORA_RANK), F32)],
        compiler_params=pltpu.CompilerParams(dimension_semantics=("parallel", "arbitrary"),
                                             vmem_limit_bytes=V7X_VMEM_LIMIT_BYTES),
        name="mla_causal_attention",
    )(q, kc)


def _dec_attn_kernel(pt_ref, q_ref, kn_ref, lat_hbm, pe_hbm, o_ref, lat0, lat1, pe0, pe1, kb_scr, sems,
                     *, layer, n_pages):
    g = pl.program_id(0)
    bufs = ((lat0, pe0), (lat1, pe1))
    past = n_pages * PAGE_SIZE

    def page_copies(seq, half, page):
        src = pt_ref[seq * n_pages + page]
        dst = pl.ds(page * PAGE_SIZE, PAGE_SIZE)
        lat_buf, pe_buf = bufs[half]
        return (pltpu.make_async_copy(lat_hbm.at[layer, src], lat_buf.at[dst, :], sems.at[0, half]),
                pltpu.make_async_copy(pe_hbm.at[layer, src], pe_buf.at[:, dst], sems.at[1, half]))

    def start_pages(seq, half, pages):
        for page in pages:
            for cp in page_copies(seq, half, page):
                cp.start()

    def wait_seq(seq, half):
        for page in range(n_pages):
            for cp in page_copies(seq, half, page):
                cp.wait()

    chunks = [slice(c, c + DEC_KEY_CHUNK) for c in range(0, past, DEC_KEY_CHUNK)]
    issue_slices = max(1, len(chunks) // 2)
    pages_per_slice = -(-n_pages // issue_slices)

    def attend(half, next_seq):
        lat_buf, pe_buf = bufs[half]
        q = q_ref[half]
        kn = kn_ref[half].astype(F32)
        ql = q[:, :KV_LORA_RANK]
        qp = q[:, KV_LORA_RANK:KV_LORA_RANK + QK_ROPE_DIM]
        parts = []
        for ci, rows in enumerate(chunks):
            if next_seq is not None:
                start_pages(next_seq, 1 - half,
                            range(min(n_pages, ci * pages_per_slice), min(n_pages, (ci + 1) * pages_per_slice)))
            lat = lat_buf[rows, :].astype(BF16)
            kb_scr[rows, :] = lat
            parts.append(_dot_nt(ql, lat) + _dot(qp, pe_buf[:, rows].astype(BF16)))
        s = jnp.concatenate(parts, axis=1)
        s_new = jnp.sum(q.astype(F32) * kn, axis=-1, keepdims=True)
        m = jnp.maximum(jnp.max(s, axis=-1, keepdims=True), s_new)
        p = jnp.exp(s - m)
        p_new = jnp.exp(s_new - m)
        denom = jnp.sum(p, axis=-1, keepdims=True) + p_new
        pb = p.astype(BF16)
        o = [p_new * kn[:, :KV_LORA_RANK], jnp.zeros((DEC_HEAD_ROWS, KV_LORA_RANK), F32)]
        for ci, rows in enumerate(chunks):
            o[ci % 2] = o[ci % 2] + _dot(pb[:, rows], kb_scr[rows, :])
        o_ref[half] = ((o[0] + o[1]) / denom).astype(BF16)

    @pl.when(g == 0)
    def _():
        start_pages(0, 0, range(n_pages))

    wait_seq(2 * g, 0)
    attend(0, 2 * g + 1)
    wait_seq(2 * g + 1, 1)
    is_last = g + 1 == pl.num_programs(0)

    @pl.when(jnp.logical_not(is_last))
    def _():
        attend(1, 2 * g + 2)

    @pl.when(is_last)
    def _():
        attend(1, None)


def _decode_attention(q, kn, cache_lat, cache_pe_t, page_table, layer):
    n_seq, n_pages = page_table.shape
    past = n_pages * PAGE_SIZE
    assert n_seq % 2 == 0 and past % DEC_KEY_CHUNK == 0
    grid_spec = pltpu.PrefetchScalarGridSpec(
        num_scalar_prefetch=1,
        grid=(n_seq // 2,),
        in_specs=[pl.BlockSpec((2, DEC_HEAD_ROWS, QK_DIM), lambda g, pt: (g, 0, 0)),
                  pl.BlockSpec((2, 1, QK_DIM), lambda g, pt: (g, 0, 0)),
                  pl.BlockSpec(memory_space=pl.ANY),
                  pl.BlockSpec(memory_space=pl.ANY)],
        out_specs=pl.BlockSpec((2, DEC_HEAD_ROWS, KV_LORA_RANK), lambda g, pt: (g, 0, 0)),
        scratch_shapes=[pltpu.VMEM((past, KV_LORA_RANK), F32), pltpu.VMEM((past, KV_LORA_RANK), F32),
                        pltpu.VMEM((QK_ROPE_DIM, past), F32), pltpu.VMEM((QK_ROPE_DIM, past), F32),
                        pltpu.VMEM((past, KV_LORA_RANK), BF16),
                        pltpu.SemaphoreType.DMA((2, 2))],
    )
    return pl.pallas_call(
        functools.partial(_dec_attn_kernel, layer=layer, n_pages=n_pages),
        out_shape=jax.ShapeDtypeStruct((n_seq, DEC_HEAD_ROWS, KV_LORA_RANK), BF16),
        grid_spec=grid_spec,
        compiler_params=pltpu.CompilerParams(dimension_semantics=("arbitrary",),
                                             vmem_limit_bytes=V7X_VMEM_LIMIT_BYTES),
        name="mla_paged_decode_attention",
    )(page_table.reshape(-1), q, kn, cache_lat, cache_pe_t)


def _mla_out_kernel(ol_ref, x_ref, mod_ref, npost_ref, wv_ref, wo_ref, xo_ref, ov_scr, *, seq_mode):
    gate = _mod_get(mod_ref, 2, seq_mode)
    for hd in range(N_HEADS):
        ov_scr[:, hd * V_HEAD_DIM:(hd + 1) * V_HEAD_DIM] = _dot(ol_ref[hd], wv_ref[hd]).astype(BF16)
    o = _dot(ov_scr[...], wo_ref[...])
    xo_ref[...] = x_ref[...] + gate * _rms(o, npost_ref[...])


def _mla_out(o_lat, x, mod, npost, w_v, w_o, *, seq_mode):
    nb, length, _ = x.shape
    tile = TOKEN_TILE if seq_mode else length
    x_spec = pl.BlockSpec((None, tile, D_MODEL), lambda b, t: (b, t, 0))
    return pl.pallas_call(
        functools.partial(_mla_out_kernel, seq_mode=seq_mode),
        out_shape=jax.ShapeDtypeStruct(x.shape, F32),
        grid=(nb, length // tile),
        in_specs=[pl.BlockSpec((None, N_HEADS, tile, KV_LORA_RANK), lambda b, t: (b, 0, t, 0)),
                  x_spec, _mod_specs(seq_mode, tile), _resident((1, D_MODEL)),
                  _resident(w_v.shape), _resident(w_o.shape)],
        out_specs=x_spec,
        scratch_shapes=[pltpu.VMEM((tile, N_HEADS * V_HEAD_DIM), BF16)],
        compiler_params=pltpu.CompilerParams(dimension_semantics=("parallel", "parallel"),
                                             vmem_limit_bytes=V7X_VMEM_LIMIT_BYTES),
        name=f"mla_out_{'seq' if seq_mode else 'tok'}",
    )(o_lat, x, mod, npost.reshape(1, -1), w_v, w_o)


def _pad_lanes(w, width):
    return jnp.pad(w, [(0, 0)] * (w.ndim - 1) + [(0, width - w.shape[-1])])


def _swap_halves(w):
    return jnp.concatenate([w[..., ROPE_HALF:], w[..., :ROPE_HALF]], axis=-1)


def _prep_mla_weights(w_qb, w_kva, w_kvb):
    qb = w_qb.reshape(Q_LORA_RANK, N_HEADS, QK_NOPE_DIM + QK_ROPE_DIM)
    q_nope = qb[..., :QK_NOPE_DIM].reshape(Q_LORA_RANK, -1)
    q_pe = qb[..., QK_NOPE_DIM:]
    w_qb2 = jnp.concatenate([q_nope,
                             _pad_lanes(q_pe, ROPE_PAD).reshape(Q_LORA_RANK, -1),
                             _pad_lanes(_swap_halves(q_pe), ROPE_PAD).reshape(Q_LORA_RANK, -1)], axis=-1)
    k_pe = w_kva[:, KV_LORA_RANK:]
    w_kva2 = jnp.concatenate([w_kva[:, :KV_LORA_RANK], _pad_lanes(k_pe, ROPE_PAD),
                              _pad_lanes(_swap_halves(k_pe), ROPE_PAD)], axis=-1)
    kvb = w_kvb.reshape(KV_LORA_RANK, N_HEADS, QK_NOPE_DIM + V_HEAD_DIM)
    w_kt = jnp.transpose(kvb[..., :QK_NOPE_DIM], (1, 2, 0))
    w_v = jnp.transpose(kvb[..., QK_NOPE_DIM:], (1, 0, 2))
    return w_qb2.astype(BF16), w_kva2.astype(BF16), w_kt.astype(BF16), w_v.astype(BF16)


def _rope_tables(pos):
    inv = ROPE_THETA ** (-jnp.arange(ROPE_HALF, dtype=F32) * 2.0 / QK_ROPE_DIM)
    ang = pos.astype(F32)[:, None] * inv[None, :]
    cos, sin = jnp.cos(ang), jnp.sin(ang)
    return (_pad_lanes(jnp.concatenate([cos, cos], axis=-1), ROPE_PAD),
            _pad_lanes(jnp.concatenate([-sin, sin], axis=-1), ROPE_PAD))


def _trunk(x, mods, pos, mix_prev, ffn_prev, attn_past, w, *, seq_mode):
    depth = len(mods)
    cos, sin = _rope_tables(pos)
    new_lat, new_pe, new_mix, new_ffn = [], [], [], []
    for i in range(depth):
        j = i // 2
        mod = mods[i]
        if i % 2 == 0:
            x, st = _gmlp(x, mod, w["norm_mix_pre"][i], w["norm_mix_post"][i], w["sc_w_in"][j], w["sc_conv_w"][j],
                          jnp.zeros((D_MODEL,), F32), w["sc_w_out"][j],
                          None if seq_mode else mix_prev[j], kind="mixer", seq_mode=seq_mode)
            new_mix.append(st if seq_mode else jnp.stack([mix_prev[j][:, 1], st], axis=1))
        else:
            q, kc, lat, pe = _mla_proj(x, mod, w["norm_mix_pre"][i], w["mla_w_qa"][j], w["mla_qa_norm"][j],
                                       w["mla_w_qb"][j], w["mla_w_kva"][j], w["mla_kv_norm"][j], w["mla_w_kt"][j],
                                       cos, sin, seq_mode=seq_mode)
            if seq_mode:
                o_lat = _attention(q, kc)
            else:
                cache_lat, cache_pe, page_table = attn_past
                qs = jnp.pad(jnp.transpose(q[0], (1, 0, 2)), ((0, 0), (0, DEC_HEAD_ROWS - N_HEADS), (0, 0)))
                o = _decode_attention(qs, jnp.transpose(kc, (1, 0, 2)), cache_lat, cache_pe, page_table, j)
                o_lat = jnp.transpose(o[:, :N_HEADS], (1, 0, 2))[None]
            x = _mla_out(o_lat, x, mod, w["norm_mix_post"][i], w["mla_w_v"][j], w["mla_w_o"][j], seq_mode=seq_mode)
            new_lat.append(lat)
            new_pe.append(pe)
        x, st = _gmlp(x, mod, w["norm_ffn_pre"][i], w["norm_ffn_post"][i], w["ffn_w_in"][i], w["ffn_conv_w"][i],
                      w["ffn_conv_b"][i], w["ffn_w_out"][i],
                      None if seq_mode else ffn_prev[i], kind="ffn", seq_mode=seq_mode)
        new_ffn.append(st if seq_mode else jnp.stack([ffn_prev[i][:, 1], st], axis=1))
    return x, jnp.stack(new_lat), jnp.stack(new_pe), jnp.stack(new_mix), jnp.stack(new_ffn)


def kernel(x_prompt, x_sample, cache_kv_latent, cache_k_rope, state_mixconv, state_ffnconv, page_table, c_prompt, c_sample, mod_w, mod_b, norm_mix_pre, norm_mix_post, norm_ffn_pre, norm_ffn_post, sc_w_in, sc_conv_w, sc_w_out, mla_w_qa, mla_qa_norm, mla_w_qb, mla_w_kva, mla_kv_norm, mla_w_kvb, mla_w_o, ffn_w_in, ffn_conv_w, ffn_conv_b, ffn_w_out):
    depth = mod_w.shape[0]
    n_attn = mla_w_qb.shape[0]
    n_prompt, seq_len, _ = x_prompt.shape
    n_sample, dec_len, _ = x_sample.shape
    assert dec_len == 1 and seq_len % TOKEN_TILE == 0 and seq_len % ATTN_T == 0

    prepped = [_prep_mla_weights(mla_w_qb[j], mla_w_kva[j], mla_w_kvb[j]) for j in range(n_attn)]
    w = {
        "norm_mix_pre": norm_mix_pre, "norm_mix_post": norm_mix_post,
        "norm_ffn_pre": norm_ffn_pre, "norm_ffn_post": norm_ffn_post,
        "sc_w_in": sc_w_in.astype(BF16), "sc_conv_w": sc_conv_w, "sc_w_out": sc_w_out.astype(BF16),
        "mla_w_qa": mla_w_qa.astype(BF16), "mla_qa_norm": mla_qa_norm, "mla_kv_norm": mla_kv_norm,
        "mla_w_qb": [p[0] for p in prepped], "mla_w_kva": [p[1] for p in prepped],
        "mla_w_kt": [p[2] for p in prepped], "mla_w_v": [p[3] for p in prepped],
        "mla_w_o": mla_w_o.astype(BF16),
        "ffn_w_in": ffn_w_in.astype(BF16), "ffn_conv_w": ffn_conv_w, "ffn_conv_b": ffn_conv_b,
        "ffn_w_out": ffn_w_out.astype(BF16),
    }

    mod_all = _modulation(jnp.concatenate([c_sample, c_prompt], axis=0), mod_w, mod_b)
    mods_s = [mod_all[i, :n_sample] for i in range(depth)]
    mods_p = [mod_all[i, n_sample:].reshape(n_prompt, N_MOD, D_MODEL) for i in range(depth)]

    pos_p = jnp.arange(seq_len, dtype=jnp.int32)
    y_p, lat_p, pe_p, mix_p, ffn_p = _trunk(x_prompt, mods_p, pos_p, None, None, None, w, seq_mode=True)

    past_len = page_table.shape[1] * cache_kv_latent.shape[2]
    pos_s = jnp.full((n_sample,), past_len, dtype=jnp.int32)
    xs = x_sample.reshape(1, n_sample, D_MODEL)
    y_s, lat_s, pe_s, mix_s, ffn_s = _trunk(xs, mods_s, pos_s, state_mixconv, state_ffnconv,
                                            (cache_kv_latent, jnp.swapaxes(cache_k_rope, 2, 3), page_table), w,
                                            seq_mode=False)

    return (y_p, y_s.reshape(n_sample, 1, D_MODEL), lat_p, pe_p,
            lat_s.reshape(n_attn, n_sample, 1, KV_LORA_RANK), pe_s.reshape(n_attn, n_sample, 1, QK_ROPE_DIM),
            mix_p, mix_s, ffn_p, ffn_s)
```

```python
import functools

import jax
import jax.numpy as jnp
from jax import lax
from jax.experimental import pallas as pl
from jax.experimental.pallas import tpu as pltpu

F32 = jnp.float32
BF16 = jnp.bfloat16

D_MODEL = 1024
N_HEADS = 8
QK_NOPE_DIM = 128
QK_ROPE_DIM = 64
ROPE_HALF = QK_ROPE_DIM // 2
V_HEAD_DIM = 128
Q_LORA_RANK = 384
KV_LORA_RANK = 256
ROPE_PAD = 128
QK_DIM = KV_LORA_RANK + ROPE_PAD
D_FF = 2816
ROPE_THETA = 10000.0
RMS_EPS = 1e-6
N_MOD = 6
ATTN_SCALE = (QK_NOPE_DIM + QK_ROPE_DIM) ** -0.5
LOG2_E = 1.4426950408889634
QUERY_SCALE = ATTN_SCALE * LOG2_E
PAGE_SIZE = 128
LANES = 128
SUBLANES = 8
NEG_BIG = float(jnp.finfo(jnp.float32).min)

V7X_VMEM_LIMIT_BYTES = 56 * 1024 * 1024
TOKEN_TILE = 1024
HIDDEN_CHUNK = 256
ATTN_T = 256
ATTN_ROW_BLOCK = 64
DEC_SEQ_PER_STEP = 4
DEC_GROUP = 2
DEC_KEY_CHUNK = 1024
DEC_HEAD_ROWS = 16


def _dot(a, b):
    return jnp.dot(a, b, preferred_element_type=F32)


def _dot_nt(a, b):
    return lax.dot_general(a, b, (((1,), (1,)), ((), ())), preferred_element_type=F32)


def _rms(x, g):
    return x * lax.rsqrt(jnp.mean(x * x, axis=-1, keepdims=True) + RMS_EPS) * g


def _silu(x):
    return x / (1.0 + jnp.exp(-x))


def _resident(shape):
    return pl.BlockSpec(shape, lambda *_: (0,) * len(shape), pipeline_mode=pl.Buffered(1))


def _weight_shape(w):
    return w[0].shape[1:] if isinstance(w, tuple) else w.shape


def _weight_array(w):
    return w[0] if isinstance(w, tuple) else w


def _weight_spec(w):
    if not isinstance(w, tuple):
        return _resident(w.shape)
    stack, layer = w
    shape = stack.shape[1:]
    return pl.BlockSpec((None,) + shape, lambda *_: (layer,) + (0,) * len(shape), pipeline_mode=pl.Buffered(1))


def _mod_specs(seq_mode, tile):
    if seq_mode:
        return pl.BlockSpec((None, N_MOD, D_MODEL), lambda b, t: (b, 0, 0))
    return pl.BlockSpec((tile, N_MOD * D_MODEL), lambda b, t: (t, 0))


def _mod_get(mod_ref, k, seq_mode):
    if seq_mode:
        return mod_ref[k:k + 1, :]
    return mod_ref[:, k * D_MODEL:(k + 1) * D_MODEL]


def _mod_kernel(c_ref, w_ref, b_ref, o_ref):
    a = _silu(c_ref[...]).astype(BF16)
    o_ref[...] = _dot(a, w_ref[...].astype(BF16)) + b_ref[...]


def _modulation(c_all, mod_w, mod_b):
    depth, _, n_out = mod_w.shape
    rows = c_all.shape[0]
    tn = 1536
    return pl.pallas_call(
        _mod_kernel,
        out_shape=jax.ShapeDtypeStruct((depth, rows, n_out), F32),
        grid=(depth, n_out // tn),
        in_specs=[pl.BlockSpec((rows, D_MODEL), lambda l, n: (0, 0)),
                  pl.BlockSpec((None, D_MODEL, tn), lambda l, n: (l, 0, n)),
                  pl.BlockSpec((None, 1, tn), lambda l, n: (l, 0, n))],
        out_specs=pl.BlockSpec((None, rows, tn), lambda l, n: (l, 0, n)),
        compiler_params=pltpu.CompilerParams(dimension_semantics=("parallel", "parallel"),
                                             vmem_limit_bytes=V7X_VMEM_LIMIT_BYTES),
        name="adaln_modulation",
    )(c_all, mod_w, mod_b.reshape(depth, 1, n_out))


def _gmlp_kernel(*refs, kind, seq_mode, tile, hidden):
    if seq_mode:
        (x_ref, mod_ref, npre_ref, npost_ref, win_ref, cw_ref, cb_ref, wout_ref,
         xo_ref, st_ref, h_scr, g_scr, pre_scr, halo_scr) = refs
    else:
        (x_ref, mod_ref, npre_ref, npost_ref, win_ref, cw_ref, cb_ref, wout_ref, p0_ref, p1_ref,
         xo_ref, st_ref, h_scr, g_scr, pre_scr) = refs
    mo = 0 if kind == "mixer" else 3
    shift, scale, gate = (_mod_get(mod_ref, mo + k, seq_mode) for k in range(3))

    x = x_ref[...]
    h_scr[...] = (_rms(x, npre_ref[...] * (1.0 + scale)) + shift).astype(BF16)

    if seq_mode:
        @pl.when(pl.program_id(1) == 0)
        def _():
            halo_scr[...] = jnp.zeros_like(halo_scr)
        top_row = lax.broadcasted_iota(jnp.int32, (SUBLANES, 1), 0)

    ck = HIDDEN_CHUNK
    n_gates = 3 if kind == "mixer" else 2

    def project_in(c, slot):
        hb = h_scr[...]
        for gi in range(n_gates):
            lo = gi * hidden + c * ck
            pre_scr[slot, gi] = _dot(hb, win_ref[:, lo:lo + ck])

    project_in(0, 0)
    for c in range(hidden // ck):
        lo = c * ck
        slot = c % 2
        if c + 1 < hidden // ck:
            project_in(c + 1, 1 - slot)
        if kind == "mixer":
            bg = pre_scr[slot, 0]
            u = pre_scr[slot, 1] * pre_scr[slot, 2]
        else:
            u = pre_scr[slot, 0]
            vg = pre_scr[slot, 1]
        w0, w1, w2 = (cw_ref[k:k + 1, lo:lo + ck] for k in range(3))
        if seq_mode:
            u1 = pltpu.roll(u, 1, 0)
            u2 = pltpu.roll(u, 2, 0)
            y = u2 * w0 + u1 * w1 + u * w2
            h0 = halo_scr[0:1, lo:lo + ck]
            h1 = halo_scr[1:2, lo:lo + ck]
            u1_top = jnp.where(top_row == 0, h1, u1[:SUBLANES])
            u2_top = jnp.where(top_row == 0, h0, jnp.where(top_row == 1, h1, u2[:SUBLANES]))
            y_top = u2_top * w0 + u1_top * w1 + u[:SUBLANES] * w2
            y = jnp.concatenate([y_top, y[SUBLANES:]], axis=0)
            last2 = u[tile - 2:tile, :]
            halo_scr[:, lo:lo + ck] = last2
            st_ref[:, lo:lo + ck] = last2
        else:
            y = p0_ref[:, lo:lo + ck] * w0 + p1_ref[:, lo:lo + ck] * w1 + u * w2
            st_ref[:, lo:lo + ck] = u
        if kind == "mixer":
            g = bg * y
        else:
            g = _silu(y + cb_ref[:, lo:lo + ck]) * vg
        g_scr[:, lo:lo + ck] = g.astype(BF16)

    o = _dot(g_scr[...], wout_ref[...])
    xo_ref[...] = x_ref[...] + _rms(o, npost_ref[...] * gate)


def _gmlp(x, mod, npre, npost, w_in, conv_w, conv_b, w_out, prev, *, kind, seq_mode):
    nb, length, _ = x.shape
    hidden = _weight_shape(w_out)[0]
    tile = TOKEN_TILE if seq_mode else length
    grid = (nb, length // tile)
    x_spec = pl.BlockSpec((None, tile, D_MODEL), lambda b, t: (b, t, 0))
    in_specs = [x_spec, _mod_specs(seq_mode, tile),
                _resident((1, D_MODEL)), _resident((1, D_MODEL)),
                _weight_spec(w_in), _resident((3, hidden)), _resident((1, hidden)), _weight_spec(w_out)]
    args = [x, mod, npre.reshape(1, -1), npost.reshape(1, -1), _weight_array(w_in), conv_w, conv_b.reshape(1, -1),
            _weight_array(w_out)]
    n_gates = 3 if kind == "mixer" else 2
    scratch = [pltpu.VMEM((tile, D_MODEL), BF16), pltpu.VMEM((tile, hidden), BF16),
               pltpu.VMEM((2, n_gates, tile, HIDDEN_CHUNK), F32)]
    if seq_mode:
        st_shape = jax.ShapeDtypeStruct((nb, 2, hidden), F32)
        st_spec = pl.BlockSpec((None, 2, hidden), lambda b, t: (b, 0, 0))
        scratch.append(pltpu.VMEM((2, hidden), F32))
    else:
        prev_spec = pl.BlockSpec((tile, hidden), lambda b, t: (t, 0))
        in_specs += [prev_spec, prev_spec]
        args += [prev[:, 0], prev[:, 1]]
        st_shape = jax.ShapeDtypeStruct((length, hidden), F32)
        st_spec = pl.BlockSpec((tile, hidden), lambda b, t: (t, 0))
    return pl.pallas_call(
        functools.partial(_gmlp_kernel, kind=kind, seq_mode=seq_mode, tile=tile, hidden=hidden),
        out_shape=(jax.ShapeDtypeStruct(x.shape, F32), st_shape),
        grid=grid,
        in_specs=in_specs,
        out_specs=(x_spec, st_spec),
        scratch_shapes=scratch,
        compiler_params=pltpu.CompilerParams(dimension_semantics=("parallel", "arbitrary"),
                                             vmem_limit_bytes=V7X_VMEM_LIMIT_BYTES),
        name=f"gated_conv_mlp_{kind}_{'seq' if seq_mode else 'tok'}",
    )(*args)


def _mla_proj_kernel(x_ref, mod_ref, npre_ref, wqa_ref, qan_ref, wqb_ref, wkva_ref, kvn_ref, wkt_ref,
                     cos_ref, sin_ref, q_out, kc_out, lat_out, pe_out, *, seq_mode):
    shift, scale = (_mod_get(mod_ref, k, seq_mode) for k in range(2))
    h = (_rms(x_ref[...], npre_ref[...] * (1.0 + scale)) + shift).astype(BF16)
    cos = cos_ref[...]
    sin = sin_ref[...]

    kv = _dot(h, wkva_ref[...])
    lat = _rms(kv[:, :KV_LORA_RANK], kvn_ref[...])
    pe = kv[:, KV_LORA_RANK:KV_LORA_RANK + ROPE_PAD] * cos + kv[:, KV_LORA_RANK + ROPE_PAD:] * sin
    lat_out[...] = lat
    pe_out[...] = pe[:, :QK_ROPE_DIM]
    kc_out[:, :KV_LORA_RANK] = lat.astype(BF16)
    kc_out[:, KV_LORA_RANK:] = pe.astype(BF16)

    qa = _rms(_dot(h, wqa_ref[...]), qan_ref[...]).astype(BF16)
    q = _dot(qa, wqb_ref[...])
    nope_w = N_HEADS * QK_NOPE_DIM
    pe_w = N_HEADS * ROPE_PAD
    for hd in range(N_HEADS):
        qn = q[:, hd * QK_NOPE_DIM:(hd + 1) * QK_NOPE_DIM].astype(BF16)
        ql = _dot(qn, wkt_ref[hd]) * QUERY_SCALE
        qp = (q[:, nope_w + hd * ROPE_PAD:nope_w + (hd + 1) * ROPE_PAD] * cos
              + q[:, nope_w + pe_w + hd * ROPE_PAD:nope_w + pe_w + (hd + 1) * ROPE_PAD] * sin) * QUERY_SCALE
        q_out[hd, :, :KV_LORA_RANK] = ql.astype(BF16)
        q_out[hd, :, KV_LORA_RANK:] = qp.astype(BF16)


def _mla_proj(x, mod, npre, w_qa, qa_norm, w_qb, w_kva, kv_norm, w_kt, cos, sin, *, seq_mode):
    nb, length, _ = x.shape
    tile = TOKEN_TILE if seq_mode else length
    row_spec = lambda w: pl.BlockSpec((None, tile, w), lambda b, t: (b, t, 0))
    tab_spec = pl.BlockSpec((tile, ROPE_PAD), lambda b, t: (t, 0))
    return pl.pallas_call(
        functools.partial(_mla_proj_kernel, seq_mode=seq_mode),
        out_shape=(jax.ShapeDtypeStruct((nb, N_HEADS, length, QK_DIM), BF16),
                   jax.ShapeDtypeStruct((nb, length, QK_DIM), BF16),
                   jax.ShapeDtypeStruct((nb, length, KV_LORA_RANK), F32),
                   jax.ShapeDtypeStruct((nb, length, QK_ROPE_DIM), F32)),
        grid=(nb, length // tile),
        in_specs=[row_spec(D_MODEL), _mod_specs(seq_mode, tile), _resident((1, D_MODEL)),
                  _weight_spec(w_qa), _resident((1, Q_LORA_RANK)), _resident(w_qb.shape),
                  _resident(w_kva.shape), _resident((1, KV_LORA_RANK)), _resident(w_kt.shape),
                  tab_spec, tab_spec],
        out_specs=(pl.BlockSpec((None, N_HEADS, tile, QK_DIM), lambda b, t: (b, 0, t, 0)),
                   row_spec(QK_DIM), row_spec(KV_LORA_RANK), row_spec(QK_ROPE_DIM)),
        compiler_params=pltpu.CompilerParams(dimension_semantics=("parallel", "parallel"),
                                             vmem_limit_bytes=V7X_VMEM_LIMIT_BYTES),
        name=f"mla_proj_{'seq' if seq_mode else 'tok'}",
    )(x, mod, npre.reshape(1, -1), _weight_array(w_qa), qa_norm.reshape(1, -1), w_qb, w_kva, kv_norm.reshape(1, -1), w_kt,
      cos, sin)


def _attn_kernel(q_ref, k_ref, o_ref, sa_scr, sb_scr, m_scr, l_scr, alpha_scr, p_scr, acc_scr):
    qi = pl.program_id(1)
    rows = N_HEADS * ATTN_T

    def key_rows(j):
        return pl.ds(pl.multiple_of(j * ATTN_T, ATTN_T), ATTN_T)

    def scores(j, s_scr):
        s_scr[...] = _dot_nt(q_ref[...].reshape(rows, QK_DIM), k_ref[key_rows(j), :])

    def update(j, s_scr, masked, first=False):
        v = k_ref[key_rows(j), :KV_LORA_RANK]
        stat_shape = (ATTN_ROW_BLOCK, LANES)
        for hd in range(N_HEADS):
            for tok0 in range(0, ATTN_T, ATTN_ROW_BLOCK):
                r = slice(hd * ATTN_T + tok0, hd * ATTN_T + tok0 + ATTN_ROW_BLOCK)
                s = s_scr[r, :]
                if masked:
                    tok = tok0 + lax.broadcasted_iota(jnp.int32, (ATTN_ROW_BLOCK, 1), 0)
                    key = lax.broadcasted_iota(jnp.int32, (1, ATTN_T), 1)
                    s = jnp.where(key <= tok, s, NEG_BIG)
                row_max = jnp.max(s, axis=-1, keepdims=True)
                if first:
                    m_new = jnp.broadcast_to(row_max, stat_shape)
                else:
                    m_prev = m_scr[r, :]
                    m_new = jnp.maximum(m_prev, row_max)
                    alpha = jnp.exp2(m_prev - m_new)
                    alpha_scr[r, :] = alpha
                p = jnp.exp2(s - jnp.tile(m_new, (1, ATTN_T // LANES)))
                row_sum = jnp.sum(p, axis=-1, keepdims=True)
                l_scr[r, :] = jnp.broadcast_to(row_sum, stat_shape) if first else alpha * l_scr[r, :] + row_sum
                m_scr[r, :] = m_new
                p_scr[r, :] = p.astype(BF16)
            r = slice(hd * ATTN_T, (hd + 1) * ATTN_T)
            pv = _dot(p_scr[r, :], v)
            if first:
                acc_scr[r, :] = pv
            else:
                acc_scr[r, :] = jnp.tile(alpha_scr[r, :], (1, KV_LORA_RANK // LANES)) * acc_scr[r, :] + pv

    @pl.when(qi == 0)
    def _():
        scores(0, sa_scr)
        update(0, sa_scr, True, first=True)

    @pl.when(qi == 1)
    def _():
        scores(0, sa_scr)
        scores(1, sb_scr)
        update(0, sa_scr, False, first=True)
        update(1, sb_scr, True)

    @pl.when(qi >= 2)
    def _():
        scores(0, sa_scr)
        scores(1, sb_scr)
        update(0, sa_scr, False, first=True)
        scores(2, sa_scr)
        update(1, sb_scr, False)

    def pair(t, carry):
        scores(2 * t + 1, sb_scr)
        update(2 * t, sa_scr, False)
        scores(2 * t + 2, sa_scr)
        update(2 * t + 1, sb_scr, False)
        return carry
    lax.fori_loop(1, qi // 2, pair, 0)

    @pl.when(jnp.logical_and(qi >= 2, qi % 2 == 0))
    def _():
        update(qi, sa_scr, True)

    @pl.when(jnp.logical_and(qi >= 2, qi % 2 == 1))
    def _():
        scores(qi, sb_scr)
        update(qi - 1, sa_scr, False)
        update(qi, sb_scr, True)

    for hd in range(N_HEADS):
        r = slice(hd * ATTN_T, (hd + 1) * ATTN_T)
        o_ref[hd] = (acc_scr[r, :] / jnp.tile(l_scr[r, :], (1, KV_LORA_RANK // LANES))).astype(BF16)


def _attention(q, kc):
    nb, _, length, _ = q.shape
    rows = N_HEADS * ATTN_T
    return pl.pallas_call(
        _attn_kernel,
        out_shape=jax.ShapeDtypeStruct((nb, N_HEADS, length, KV_LORA_RANK), BF16),
        grid=(nb, length // ATTN_T),
        in_specs=[pl.BlockSpec((None, N_HEADS, ATTN_T, QK_DIM), lambda b, i: (b, 0, i, 0)),
                  pl.BlockSpec((None, length, QK_DIM), lambda b, i: (b, 0, 0))],
        out_specs=pl.BlockSpec((None, N_HEADS, ATTN_T, KV_LORA_RANK), lambda b, i: (b, 0, i, 0)),
        scratch_shapes=[pltpu.VMEM((rows, ATTN_T), F32), pltpu.VMEM((rows, ATTN_T), F32),
                        pltpu.VMEM((rows, LANES), F32), pltpu.VMEM((rows, LANES), F32),
                        pltpu.VMEM((rows, LANES), F32), pltpu.VMEM((rows, ATTN_T), BF16),
                        pltpu.VMEM((rows, KV_LORA_RANK), F32)],
        compiler_params=pltpu.CompilerParams(dimension_semantics=("parallel", "arbitrary"),
                                             vmem_limit_bytes=V7X_VMEM_LIMIT_BYTES),
        name="mla_causal_attention",
    )(q, kc)


def _dec_attn_kernel(pt_ref, q_ref, kn_ref, lat_hbm, pe_hbm, o_ref, *scratch, layer, n_pages):
    g = pl.program_id(0)
    lat_bufs = scratch[:DEC_SEQ_PER_STEP]
    pe_bufs = scratch[DEC_SEQ_PER_STEP:2 * DEC_SEQ_PER_STEP]
    kb_scrs = scratch[2 * DEC_SEQ_PER_STEP:2 * DEC_SEQ_PER_STEP + DEC_GROUP]
    sems = scratch[-1]
    past = n_pages * PAGE_SIZE

    def page_copies(seq, buf, page):
        src = pt_ref[seq * n_pages + page]
        dst = pl.ds(page * PAGE_SIZE, PAGE_SIZE)
        return (pltpu.make_async_copy(lat_hbm.at[layer, src], lat_bufs[buf].at[dst, :], sems.at[0, buf]),
                pltpu.make_async_copy(pe_hbm.at[layer, src], pe_bufs[buf].at[:, dst], sems.at[1, buf]))

    def start_seq(seq, buf):
        for page in range(n_pages):
            for cp in page_copies(seq, buf, page):
                cp.start(priority=page % 2)

    def wait_seq(seq, buf):
        for page in range(n_pages):
            for cp in page_copies(seq, buf, page):
                cp.wait()

    chunks = [slice(c, c + DEC_KEY_CHUNK) for c in range(0, past, DEC_KEY_CHUNK)]

    def attend(bufs):
        n = len(bufs)
        q = [q_ref[b] for b in bufs]
        kn = [kn_ref[b].astype(F32) for b in bufs]
        parts = [[] for _ in bufs]
        for rows in chunks:
            for i, b in enumerate(bufs):
                lat = lat_bufs[b][rows, :].astype(BF16)
                kb_scrs[i][rows, :] = lat
                parts[i].append(_dot_nt(q[i][:, :KV_LORA_RANK], lat)
                                + _dot(q[i][:, KV_LORA_RANK:KV_LORA_RANK + QK_ROPE_DIM],
                                       pe_bufs[b][:, rows].astype(BF16)))
        pb, o, denom = [], [], []
        for i in range(n):
            s = jnp.concatenate(parts[i], axis=1)
            s_new = jnp.sum(q[i].astype(F32) * kn[i], axis=-1, keepdims=True)
            m = jnp.maximum(jnp.max(s, axis=-1, keepdims=True), s_new)
            p = jnp.exp2(s - m)
            p_new = jnp.exp2(s_new - m)
            denom.append(jnp.sum(p, axis=-1, keepdims=True) + p_new)
            pb.append(p.astype(BF16))
            o.append(p_new * kn[i][:, :KV_LORA_RANK])
        for rows in chunks:
            for i in range(n):
                o[i] = o[i] + _dot(pb[i][:, rows], kb_scrs[i][rows, :])
        for i, b in enumerate(bufs):
            o_ref[b] = (o[i] / denom[i]).astype(BF16)

    groups = [tuple(range(k, k + DEC_GROUP)) for k in range(0, DEC_SEQ_PER_STEP, DEC_GROUP)]

    @pl.when(g == 0)
    def _():
        for k in range(DEC_SEQ_PER_STEP):
            start_seq(k, k)

    for bufs in groups:
        for b in bufs:
            wait_seq(DEC_SEQ_PER_STEP * g + b, b)
        attend(bufs)

        @pl.when(g + 1 < pl.num_programs(0))
        def _():
            for b in bufs:
                start_seq(DEC_SEQ_PER_STEP * (g + 1) + b, b)


def _decode_attention(q, kn, cache_lat, cache_pe_t, page_table, layer):
    n_seq, n_pages = page_table.shape
    past = n_pages * PAGE_SIZE
    n_buf = DEC_SEQ_PER_STEP
    assert n_seq % n_buf == 0 and past % DEC_KEY_CHUNK == 0 and n_buf % DEC_GROUP == 0 and n_buf > DEC_GROUP
    grid_spec = pltpu.PrefetchScalarGridSpec(
        num_scalar_prefetch=1,
        grid=(n_seq // n_buf,),
        in_specs=[pl.BlockSpec((n_buf, DEC_HEAD_ROWS, QK_DIM), lambda g, pt: (g, 0, 0)),
                  pl.BlockSpec((n_buf, 1, QK_DIM), lambda g, pt: (g, 0, 0)),
                  pl.BlockSpec(memory_space=pl.ANY),
                  pl.BlockSpec(memory_space=pl.ANY)],
        out_specs=pl.BlockSpec((n_buf, DEC_HEAD_ROWS, KV_LORA_RANK), lambda g, pt: (g, 0, 0)),
        scratch_shapes=([pltpu.VMEM((past, KV_LORA_RANK), F32)] * n_buf
                        + [pltpu.VMEM((QK_ROPE_DIM, past), F32)] * n_buf
                        + [pltpu.VMEM((past, KV_LORA_RANK), BF16)] * DEC_GROUP
                        + [pltpu.SemaphoreType.DMA((2, n_buf))]),
    )
    return pl.pallas_call(
        functools.partial(_dec_attn_kernel, layer=layer, n_pages=n_pages),
        out_shape=jax.ShapeDtypeStruct((n_seq, DEC_HEAD_ROWS, KV_LORA_RANK), BF16),
        grid_spec=grid_spec,
        compiler_params=pltpu.CompilerParams(dimension_semantics=("arbitrary",),
                                             vmem_limit_bytes=V7X_VMEM_LIMIT_BYTES),
        name="mla_paged_decode_attention",
    )(page_table.reshape(-1), q, kn, cache_lat, cache_pe_t)


def _mla_out_kernel(ol_ref, x_ref, mod_ref, npost_ref, wv_ref, wo_ref, xo_ref, ov_scr, *, seq_mode):
    gate = _mod_get(mod_ref, 2, seq_mode)
    for hd in range(N_HEADS):
        ov_scr[:, hd * V_HEAD_DIM:(hd + 1) * V_HEAD_DIM] = _dot(ol_ref[hd], wv_ref[hd]).astype(BF16)
    o = _dot(ov_scr[...], wo_ref[...])
    xo_ref[...] = x_ref[...] + _rms(o, npost_ref[...] * gate)


def _mla_out(o_lat, x, mod, npost, w_v, w_o, *, seq_mode):
    nb, length, _ = x.shape
    tile = TOKEN_TILE if seq_mode else length
    x_spec = pl.BlockSpec((None, tile, D_MODEL), lambda b, t: (b, t, 0))
    return pl.pallas_call(
        functools.partial(_mla_out_kernel, seq_mode=seq_mode),
        out_shape=jax.ShapeDtypeStruct(x.shape, F32),
        grid=(nb, length // tile),
        in_specs=[pl.BlockSpec((None, N_HEADS, tile, KV_LORA_RANK), lambda b, t: (b, 0, t, 0)),
                  x_spec, _mod_specs(seq_mode, tile), _resident((1, D_MODEL)),
                  _resident(w_v.shape), _weight_spec(w_o)],
        out_specs=x_spec,
        scratch_shapes=[pltpu.VMEM((tile, N_HEADS * V_HEAD_DIM), BF16)],
        compiler_params=pltpu.CompilerParams(dimension_semantics=("parallel", "parallel"),
                                             vmem_limit_bytes=V7X_VMEM_LIMIT_BYTES),
        name=f"mla_out_{'seq' if seq_mode else 'tok'}",
    )(o_lat, x, mod, npost.reshape(1, -1), w_v, _weight_array(w_o))


def _pad_lanes(w, width):
    return jnp.pad(w, [(0, 0)] * (w.ndim - 1) + [(0, width - w.shape[-1])])


def _swap_halves(w):
    return jnp.concatenate([w[..., ROPE_HALF:], w[..., :ROPE_HALF]], axis=-1)


def _prep_mla_weights(w_qb, w_kva, w_kvb):
    qb = w_qb.reshape(Q_LORA_RANK, N_HEADS, QK_NOPE_DIM + QK_ROPE_DIM)
    q_nope = qb[..., :QK_NOPE_DIM].reshape(Q_LORA_RANK, -1)
    q_pe = qb[..., QK_NOPE_DIM:]
    w_qb2 = jnp.concatenate([q_nope,
                             _pad_lanes(q_pe, ROPE_PAD).reshape(Q_LORA_RANK, -1),
                             _pad_lanes(_swap_halves(q_pe), ROPE_PAD).reshape(Q_LORA_RANK, -1)], axis=-1)
    k_pe = w_kva[:, KV_LORA_RANK:]
    w_kva2 = jnp.concatenate([w_kva[:, :KV_LORA_RANK], _pad_lanes(k_pe, ROPE_PAD),
                              _pad_lanes(_swap_halves(k_pe), ROPE_PAD)], axis=-1)
    kvb = w_kvb.reshape(KV_LORA_RANK, N_HEADS, QK_NOPE_DIM + V_HEAD_DIM)
    w_kt = jnp.transpose(kvb[..., :QK_NOPE_DIM], (1, 2, 0))
    w_v = jnp.transpose(kvb[..., QK_NOPE_DIM:], (1, 0, 2))
    return w_qb2.astype(BF16), w_kva2.astype(BF16), w_kt.astype(BF16), w_v.astype(BF16)


def _rope_tables(pos):
    inv = ROPE_THETA ** (-jnp.arange(ROPE_HALF, dtype=F32) * 2.0 / QK_ROPE_DIM)
    ang = pos.astype(F32)[:, None] * inv[None, :]
    cos, sin = jnp.cos(ang), jnp.sin(ang)
    return (_pad_lanes(jnp.concatenate([cos, cos], axis=-1), ROPE_PAD),
            _pad_lanes(jnp.concatenate([-sin, sin], axis=-1), ROPE_PAD))


def _trunk(x, mods, pos, mix_prev, ffn_prev, attn_past, w, *, seq_mode):
    depth = len(mods)
    cos, sin = _rope_tables(pos)
    new_lat, new_pe, new_mix, new_ffn = [], [], [], []
    for i in range(depth):
        j = i // 2
        mod = mods[i]
        if i % 2 == 0:
            x, st = _gmlp(x, mod, w["norm_mix_pre"][i], w["norm_mix_post"][i], (w["sc_w_in"], j), w["sc_conv_w"][j],
                          jnp.zeros((D_MODEL,), F32), (w["sc_w_out"], j),
                          None if seq_mode else mix_prev[j], kind="mixer", seq_mode=seq_mode)
            new_mix.append(st if seq_mode else jnp.stack([mix_prev[j][:, 1], st], axis=1))
        else:
            q, kc, lat, pe = _mla_proj(x, mod, w["norm_mix_pre"][i], (w["mla_w_qa"], j), w["mla_qa_norm"][j],
                                       w["mla_w_qb"][j], w["mla_w_kva"][j], w["mla_kv_norm"][j], w["mla_w_kt"][j],
                                       cos, sin, seq_mode=seq_mode)
            if seq_mode:
                o_lat = _attention(q, kc)
            else:
                cache_lat, cache_pe, page_table = attn_past
                qs = jnp.pad(jnp.transpose(q[0], (1, 0, 2)), ((0, 0), (0, DEC_HEAD_ROWS - N_HEADS), (0, 0)))
                o = _decode_attention(qs, jnp.transpose(kc, (1, 0, 2)), cache_lat, cache_pe, page_table, j)
                o_lat = jnp.transpose(o[:, :N_HEADS], (1, 0, 2))[None]
            x = _mla_out(o_lat, x, mod, w["norm_mix_post"][i], w["mla_w_v"][j], (w["mla_w_o"], j), seq_mode=seq_mode)
            new_lat.append(lat)
            new_pe.append(pe)
        x, st = _gmlp(x, mod, w["norm_ffn_pre"][i], w["norm_ffn_post"][i], (w["ffn_w_in"], i), w["ffn_conv_w"][i],
                      w["ffn_conv_b"][i], (w["ffn_w_out"], i),
                      None if seq_mode else ffn_prev[i], kind="ffn", seq_mode=seq_mode)
        new_ffn.append(st if seq_mode else jnp.stack([ffn_prev[i][:, 1], st], axis=1))
    return x, jnp.stack(new_lat), jnp.stack(new_pe), jnp.stack(new_mix), jnp.stack(new_ffn)


def kernel(x_prompt, x_sample, cache_kv_latent, cache_k_rope, state_mixconv, state_ffnconv, page_table, c_prompt, c_sample, mod_w, mod_b, norm_mix_pre, norm_mix_post, norm_ffn_pre, norm_ffn_post, sc_w_in, sc_conv_w, sc_w_out, mla_w_qa, mla_qa_norm, mla_w_qb, mla_w_kva, mla_kv_norm, mla_w_kvb, mla_w_o, ffn_w_in, ffn_conv_w, ffn_conv_b, ffn_w_out):
    depth = mod_w.shape[0]
    n_attn = mla_w_qb.shape[0]
    n_prompt, seq_len, _ = x_prompt.shape
    n_sample, dec_len, _ = x_sample.shape
    assert dec_len == 1 and seq_len % TOKEN_TILE == 0 and seq_len % ATTN_T == 0

    prepped = [_prep_mla_weights(mla_w_qb[j], mla_w_kva[j], mla_w_kvb[j]) for j in range(n_attn)]
    w = {
        "norm_mix_pre": norm_mix_pre, "norm_mix_post": norm_mix_post,
        "norm_ffn_pre": norm_ffn_pre, "norm_ffn_post": norm_ffn_post,
        "sc_w_in": sc_w_in.astype(BF16), "sc_conv_w": sc_conv_w, "sc_w_out": sc_w_out.astype(BF16),
        "mla_w_qa": mla_w_qa.astype(BF16), "mla_qa_norm": mla_qa_norm, "mla_kv_norm": mla_kv_norm,
        "mla_w_qb": [p[0] for p in prepped], "mla_w_kva": [p[1] for p in prepped],
        "mla_w_kt": [p[2] for p in prepped], "mla_w_v": [p[3] for p in prepped],
        "mla_w_o": mla_w_o.astype(BF16),
        "ffn_w_in": ffn_w_in.astype(BF16), "ffn_conv_w": ffn_conv_w, "ffn_conv_b": ffn_conv_b,
        "ffn_w_out": ffn_w_out.astype(BF16),
    }

    mod_all = _modulation(jnp.concatenate([c_sample, c_prompt], axis=0), mod_w, mod_b)
    mods_s = [mod_all[i, :n_sample] for i in range(depth)]
    mods_p = [mod_all[i, n_sample:].reshape(n_prompt, N_MOD, D_MODEL) for i in range(depth)]

    pos_p = jnp.arange(seq_len, dtype=jnp.int32)
    y_p, lat_p, pe_p, mix_p, ffn_p = _trunk(x_prompt, mods_p, pos_p, None, None, None, w, seq_mode=True)

    past_len = page_table.shape[1] * cache_kv_latent.shape[2]
    pos_s = jnp.full((n_sample,), past_len, dtype=jnp.int32)
    xs = x_sample.reshape(1, n_sample, D_MODEL)
    y_s, lat_s, pe_s, mix_s, ffn_s = _trunk(xs, mods_s, pos_s, state_mixconv, state_ffnconv,
                                            (cache_kv_latent, jnp.swapaxes(cache_k_rope, 2, 3), page_table), w,
                                            seq_mode=False)

    return (y_p, y_s.reshape(n_sample, 1, D_MODEL), lat_p, pe_p,
            lat_s.reshape(n_attn, n_sample, 1, KV_LORA_RANK), pe_s.reshape(n_attn, n_sample, 1, QK_ROPE_DIM),
            mix_p, mix_s, ffn_p, ffn_s)
```

```python
import functools

import jax
import jax.numpy as jnp
from jax import lax
from jax.experimental import pallas as pl
from jax.experimental.pallas import tpu as pltpu

F32 = jnp.float32
BF16 = jnp.bfloat16

D_MODEL = 1024
N_HEADS = 8
QK_NOPE_DIM = 128
QK_ROPE_DIM = 64
ROPE_HALF = QK_ROPE_DIM // 2
V_HEAD_DIM = 128
Q_LORA_RANK = 384
KV_LORA_RANK = 256
ROPE_PAD = 128
QK_DIM = KV_LORA_RANK + ROPE_PAD
D_FF = 2816
ROPE_THETA = 10000.0
RMS_EPS = 1e-6
N_MOD = 6
ATTN_SCALE = (QK_NOPE_DIM + QK_ROPE_DIM) ** -0.5
LOG2_E = 1.4426950408889634
QUERY_SCALE = ATTN_SCALE * LOG2_E
PAGE_SIZE = 128
LANES = 128
SUBLANES = 8
NEG_BIG = float(jnp.finfo(jnp.float32).min)

V7X_VMEM_LIMIT_BYTES = 56 * 1024 * 1024
TOKEN_TILE = 1024
HIDDEN_CHUNK = 256
ATTN_T = 256
ATTN_ROW_BLOCK = 64
DEC_SEQ_PER_STEP = 4
DEC_GROUP = 2
DEC_KEY_CHUNK = 1024
DEC_HEAD_ROWS = 16


def _dot(a, b):
    return jnp.dot(a, b, preferred_element_type=F32)


def _dot_nt(a, b):
    return lax.dot_general(a, b, (((1,), (1,)), ((), ())), preferred_element_type=F32)


def _rms(x, g):
    return x * lax.rsqrt(jnp.mean(x * x, axis=-1, keepdims=True) + RMS_EPS) * g


def _silu(x):
    return x / (1.0 + jnp.exp(-x))


def _resident(shape):
    return pl.BlockSpec(shape, lambda *_: (0,) * len(shape), pipeline_mode=pl.Buffered(1))


def _weight_shape(w):
    return w[0].shape[1:] if isinstance(w, tuple) else w.shape


def _weight_array(w):
    return w[0] if isinstance(w, tuple) else w


def _weight_spec(w):
    if not isinstance(w, tuple):
        return _resident(w.shape)
    stack, layer = w
    shape = stack.shape[1:]
    return pl.BlockSpec((None,) + shape, lambda *_: (layer,) + (0,) * len(shape), pipeline_mode=pl.Buffered(1))


def _mod_specs(seq_mode, tile):
    if seq_mode:
        return pl.BlockSpec((None, N_MOD, D_MODEL), lambda b, t: (b, 0, 0))
    return pl.BlockSpec((tile, N_MOD * D_MODEL), lambda b, t: (t, 0))


def _mod_get(mod_ref, k, seq_mode):
    if seq_mode:
        return mod_ref[k:k + 1, :]
    return mod_ref[:, k * D_MODEL:(k + 1) * D_MODEL]


def _mod_kernel(c_ref, w_ref, b_ref, o_ref):
    a = _silu(c_ref[...]).astype(BF16)
    o_ref[...] = _dot(a, w_ref[...].astype(BF16)) + b_ref[...]


def _modulation(c_all, mod_w, mod_b):
    depth, _, n_out = mod_w.shape
    rows = c_all.shape[0]
    tn = 1536
    return pl.pallas_call(
        _mod_kernel,
        out_shape=jax.ShapeDtypeStruct((depth, rows, n_out), F32),
        grid=(depth, n_out // tn),
        in_specs=[pl.BlockSpec((rows, D_MODEL), lambda l, n: (0, 0)),
                  pl.BlockSpec((None, D_MODEL, tn), lambda l, n: (l, 0, n)),
                  pl.BlockSpec((None, 1, tn), lambda l, n: (l, 0, n))],
        out_specs=pl.BlockSpec((None, rows, tn), lambda l, n: (l, 0, n)),
        compiler_params=pltpu.CompilerParams(dimension_semantics=("parallel", "parallel"),
                                             vmem_limit_bytes=V7X_VMEM_LIMIT_BYTES),
        name="adaln_modulation",
    )(c_all, mod_w, mod_b.reshape(depth, 1, n_out))


def _gmlp_kernel(*refs, kind, seq_mode, tile, hidden):
    if seq_mode:
        (x_ref, mod_ref, npre_ref, npost_ref, win_ref, cw_ref, cb_ref, wout_ref,
         xo_ref, st_ref, h_scr, g_scr, pre_scr, halo_scr) = refs
    else:
        (x_ref, mod_ref, npre_ref, npost_ref, win_ref, cw_ref, cb_ref, wout_ref, p0_ref, p1_ref,
         xo_ref, st_ref, h_scr, g_scr, pre_scr) = refs
    mo = 0 if kind == "mixer" else 3
    shift, scale, gate = (_mod_get(mod_ref, mo + k, seq_mode) for k in range(3))

    x = x_ref[...]
    h_scr[...] = (_rms(x, npre_ref[...] * (1.0 + scale)) + shift).astype(BF16)

    if seq_mode:
        @pl.when(pl.program_id(1) == 0)
        def _():
            halo_scr[...] = jnp.zeros_like(halo_scr)
        top_row = lax.broadcasted_iota(jnp.int32, (SUBLANES, 1), 0)

    ck = HIDDEN_CHUNK
    n_gates = 3 if kind == "mixer" else 2

    def project_in(c, slot):
        hb = h_scr[...]
        for gi in range(n_gates):
            lo = gi * hidden + c * ck
            pre_scr[slot, gi] = _dot(hb, win_ref[:, lo:lo + ck])

    project_in(0, 0)
    for c in range(hidden // ck):
        lo = c * ck
        slot = c % 2
        if c + 1 < hidden // ck:
            project_in(c + 1, 1 - slot)
        if kind == "mixer":
            bg = pre_scr[slot, 0]
            u = pre_scr[slot, 1] * pre_scr[slot, 2]
        else:
            u = pre_scr[slot, 0]
            vg = pre_scr[slot, 1]
        w0, w1, w2 = (cw_ref[k:k + 1, lo:lo + ck] for k in range(3))
        if seq_mode:
            u1 = pltpu.roll(u, 1, 0)
            u2 = pltpu.roll(u, 2, 0)
            y = u2 * w0 + u1 * w1 + u * w2
            h0 = halo_scr[0:1, lo:lo + ck]
            h1 = halo_scr[1:2, lo:lo + ck]
            u1_top = jnp.where(top_row == 0, h1, u1[:SUBLANES])
            u2_top = jnp.where(top_row == 0, h0, jnp.where(top_row == 1, h1, u2[:SUBLANES]))
            y_top = u2_top * w0 + u1_top * w1 + u[:SUBLANES] * w2
            y = jnp.concatenate([y_top, y[SUBLANES:]], axis=0)
            last2 = u[tile - 2:tile, :]
            halo_scr[:, lo:lo + ck] = last2
            st_ref[:, lo:lo + ck] = last2
        else:
            y = p0_ref[:, lo:lo + ck] * w0 + p1_ref[:, lo:lo + ck] * w1 + u * w2
            st_ref[:, lo:lo + ck] = u
        if kind == "mixer":
            g = bg * y
        else:
            g = _silu(y + cb_ref[:, lo:lo + ck]) * vg
        g_scr[:, lo:lo + ck] = g.astype(BF16)

    o = _dot(g_scr[...], wout_ref[...])
    xo_ref[...] = x_ref[...] + _rms(o, npost_ref[...] * gate)


def _gmlp(x, mod, npre, npost, w_in, conv_w, conv_b, w_out, prev, *, kind, seq_mode):
    nb, length, _ = x.shape
    hidden = _weight_shape(w_out)[0]
    tile = TOKEN_TILE if seq_mode else length
    grid = (nb, length // tile)
    x_spec = pl.BlockSpec((None, tile, D_MODEL), lambda b, t: (b, t, 0))
    in_specs = [x_spec, _mod_specs(seq_mode, tile),
                _resident((1, D_MODEL)), _resident((1, D_MODEL)),
                _weight_spec(w_in), _resident((3, hidden)), _resident((1, hidden)), _weight_spec(w_out)]
    args = [x, mod, npre.reshape(1, -1), npost.reshape(1, -1), _weight_array(w_in), conv_w, conv_b.reshape(1, -1),
            _weight_array(w_out)]
    n_gates = 3 if kind == "mixer" else 2
    scratch = [pltpu.VMEM((tile, D_MODEL), BF16), pltpu.VMEM((tile, hidden), BF16),
               pltpu.VMEM((2, n_gates, tile, HIDDEN_CHUNK), F32)]
    if seq_mode:
        st_shape = jax.ShapeDtypeStruct((nb, 2, hidden), F32)
        st_spec = pl.BlockSpec((None, 2, hidden), lambda b, t: (b, 0, 0))
        scratch.append(pltpu.VMEM((2, hidden), F32))
    else:
        prev_spec = pl.BlockSpec((tile, hidden), lambda b, t: (t, 0))
        in_specs += [prev_spec, prev_spec]
        args += [prev[:, 0], prev[:, 1]]
        st_shape = jax.ShapeDtypeStruct((length, hidden), F32)
        st_spec = pl.BlockSpec((tile, hidden), lambda b, t: (t, 0))
    return pl.pallas_call(
        functools.partial(_gmlp_kernel, kind=kind, seq_mode=seq_mode, tile=tile, hidden=hidden),
        out_shape=(jax.ShapeDtypeStruct(x.shape, F32), st_shape),
        grid=grid,
        in_specs=in_specs,
        out_specs=(x_spec, st_spec),
        scratch_shapes=scratch,
        compiler_params=pltpu.CompilerParams(dimension_semantics=("parallel", "arbitrary"),
                                             vmem_limit_bytes=V7X_VMEM_LIMIT_BYTES),
        name=f"gated_conv_mlp_{kind}_{'seq' if seq_mode else 'tok'}",
    )(*args)


def _mla_proj_kernel(x_ref, mod_ref, npre_ref, wqa_ref, qan_ref, wqb_ref, wkva_ref, kvn_ref, wkt_ref,
                     cos_ref, sin_ref, q_out, kc_out, lat_out, pe_out, *, seq_mode):
    shift, scale = (_mod_get(mod_ref, k, seq_mode) for k in range(2))
    h = (_rms(x_ref[...], npre_ref[...] * (1.0 + scale)) + shift).astype(BF16)
    cos = cos_ref[...]
    sin = sin_ref[...]

    kv = _dot(h, wkva_ref[...])
    lat = _rms(kv[:, :KV_LORA_RANK], kvn_ref[...])
    pe = kv[:, KV_LORA_RANK:KV_LORA_RANK + ROPE_PAD] * cos + kv[:, KV_LORA_RANK + ROPE_PAD:] * sin
    lat_out[...] = lat
    pe_out[...] = pe[:, :QK_ROPE_DIM]
    kc_out[:, :KV_LORA_RANK] = lat.astype(BF16)
    kc_out[:, KV_LORA_RANK:] = pe.astype(BF16)

    qa = _rms(_dot(h, wqa_ref[...]), qan_ref[...]).astype(BF16)
    q = _dot(qa, wqb_ref[...])
    nope_w = N_HEADS * QK_NOPE_DIM
    pe_w = N_HEADS * ROPE_PAD
    for hd in range(N_HEADS):
        qn = q[:, hd * QK_NOPE_DIM:(hd + 1) * QK_NOPE_DIM].astype(BF16)
        ql = _dot(qn, wkt_ref[hd]) * QUERY_SCALE
        qp = (q[:, nope_w + hd * ROPE_PAD:nope_w + (hd + 1) * ROPE_PAD] * cos
              + q[:, nope_w + pe_w + hd * ROPE_PAD:nope_w + pe_w + (hd + 1) * ROPE_PAD] * sin) * QUERY_SCALE
        q_out[hd, :, :KV_LORA_RANK] = ql.astype(BF16)
        q_out[hd, :, KV_LORA_RANK:] = qp.astype(BF16)


def _mla_proj(x, mod, npre, w_qa, qa_norm, w_qb, w_kva, kv_norm, w_kt, cos, sin, *, seq_mode):
    nb, length, _ = x.shape
    tile = TOKEN_TILE if seq_mode else length
    row_spec = lambda w: pl.BlockSpec((None, tile, w), lambda b, t: (b, t, 0))
    tab_spec = pl.BlockSpec((tile, ROPE_PAD), lambda b, t: (t, 0))
    return pl.pallas_call(
        functools.partial(_mla_proj_kernel, seq_mode=seq_mode),
        out_shape=(jax.ShapeDtypeStruct((nb, N_HEADS, length, QK_DIM), BF16),
                   jax.ShapeDtypeStruct((nb, length, QK_DIM), BF16),
                   jax.ShapeDtypeStruct((nb, length, KV_LORA_RANK), F32),
                   jax.ShapeDtypeStruct((nb, length, QK_ROPE_DIM), F32)),
        grid=(nb, length // tile),
        in_specs=[row_spec(D_MODEL), _mod_specs(seq_mode, tile), _resident((1, D_MODEL)),
                  _weight_spec(w_qa), _resident((1, Q_LORA_RANK)), _resident(w_qb.shape),
                  _resident(w_kva.shape), _resident((1, KV_LORA_RANK)), _resident(w_kt.shape),
                  tab_spec, tab_spec],
        out_specs=(pl.BlockSpec((None, N_HEADS, tile, QK_DIM), lambda b, t: (b, 0, t, 0)),
                   row_spec(QK_DIM), row_spec(KV_LORA_RANK), row_spec(QK_ROPE_DIM)),
        compiler_params=pltpu.CompilerParams(dimension_semantics=("parallel", "parallel"),
                                             vmem_limit_bytes=V7X_VMEM_LIMIT_BYTES),
        name=f"mla_proj_{'seq' if seq_mode else 'tok'}",
    )(x, mod, npre.reshape(1, -1), _weight_array(w_qa), qa_norm.reshape(1, -1), w_qb, w_kva, kv_norm.reshape(1, -1), w_kt,
      cos, sin)


def _attn_kernel(q_ref, k_ref, o_ref, sa_scr, sb_scr, m_scr, l_scr, alpha_scr, p_scr, acc_scr):
    qi = pl.program_id(1)
    rows = N_HEADS * ATTN_T

    def key_rows(j):
        return pl.ds(pl.multiple_of(j * ATTN_T, ATTN_T), ATTN_T)

    def scores(j, s_scr):
        s_scr[...] = _dot_nt(q_ref[...].reshape(rows, QK_DIM), k_ref[key_rows(j), :])

    def update(j, s_scr, masked, first=False):
        v = k_ref[key_rows(j), :KV_LORA_RANK]
        stat_shape = (ATTN_ROW_BLOCK, LANES)
        for hd in range(N_HEADS):
            for tok0 in range(0, ATTN_T, ATTN_ROW_BLOCK):
                r = slice(hd * ATTN_T + tok0, hd * ATTN_T + tok0 + ATTN_ROW_BLOCK)
                s = s_scr[r, :]
                if masked:
                    tok = tok0 + lax.broadcasted_iota(jnp.int32, (ATTN_ROW_BLOCK, 1), 0)
                    key = lax.broadcasted_iota(jnp.int32, (1, ATTN_T), 1)
                    s = jnp.where(key <= tok, s, NEG_BIG)
                row_max = jnp.max(s, axis=-1, keepdims=True)
                if first:
                    m_new = jnp.broadcast_to(row_max, stat_shape)
                else:
                    m_prev = m_scr[r, :]
                    m_new = jnp.maximum(m_prev, row_max)
                    alpha = jnp.exp2(m_prev - m_new)
                    alpha_scr[r, :] = alpha
                p = jnp.exp2(s - jnp.tile(m_new, (1, ATTN_T // LANES)))
                row_sum = jnp.sum(p, axis=-1, keepdims=True)
                l_scr[r, :] = jnp.broadcast_to(row_sum, stat_shape) if first else alpha * l_scr[r, :] + row_sum
                m_scr[r, :] = m_new
                p_scr[r, :] = p.astype(BF16)
        pv = _dot(p_scr[...], v)
        if first:
            acc_scr[...] = pv
        else:
            acc_scr[...] = jnp.tile(alpha_scr[...], (1, KV_LORA_RANK // LANES)) * acc_scr[...] + pv

    @pl.when(qi == 0)
    def _():
        scores(0, sa_scr)
        update(0, sa_scr, True, first=True)

    @pl.when(qi == 1)
    def _():
        scores(0, sa_scr)
        scores(1, sb_scr)
        update(0, sa_scr, False, first=True)
        update(1, sb_scr, True)

    @pl.when(qi >= 2)
    def _():
        scores(0, sa_scr)
        scores(1, sb_scr)
        update(0, sa_scr, False, first=True)
        scores(2, sa_scr)
        update(1, sb_scr, False)

    def pair(t, carry):
        scores(2 * t + 1, sb_scr)
        update(2 * t, sa_scr, False)
        scores(2 * t + 2, sa_scr)
        update(2 * t + 1, sb_scr, False)
        return carry
    lax.fori_loop(1, qi // 2, pair, 0)

    @pl.when(jnp.logical_and(qi >= 2, qi % 2 == 0))
    def _():
        update(qi, sa_scr, True)

    @pl.when(jnp.logical_and(qi >= 2, qi % 2 == 1))
    def _():
        scores(qi, sb_scr)
        update(qi - 1, sa_scr, False)
        update(qi, sb_scr, True)

    for hd in range(N_HEADS):
        r = slice(hd * ATTN_T, (hd + 1) * ATTN_T)
        o_ref[hd] = (acc_scr[r, :] / jnp.tile(l_scr[r, :], (1, KV_LORA_RANK // LANES))).astype(BF16)


def _attention(q, kc):
    nb, _, length, _ = q.shape
    rows = N_HEADS * ATTN_T
    return pl.pallas_call(
        _attn_kernel,
        out_shape=jax.ShapeDtypeStruct((nb, N_HEADS, length, KV_LORA_RANK), BF16),
        grid=(nb, length // ATTN_T),
        in_specs=[pl.BlockSpec((None, N_HEADS, ATTN_T, QK_DIM), lambda b, i: (b, 0, i, 0)),
                  pl.BlockSpec((None, length, QK_DIM), lambda b, i: (b, 0, 0))],
        out_specs=pl.BlockSpec((None, N_HEADS, ATTN_T, KV_LORA_RANK), lambda b, i: (b, 0, i, 0)),
        scratch_shapes=[pltpu.VMEM((rows, ATTN_T), F32), pltpu.VMEM((rows, ATTN_T), F32),
                        pltpu.VMEM((rows, LANES), F32), pltpu.VMEM((rows, LANES), F32),
                        pltpu.VMEM((rows, LANES), F32), pltpu.VMEM((rows, ATTN_T), BF16),
                        pltpu.VMEM((rows, KV_LORA_RANK), F32)],
        compiler_params=pltpu.CompilerParams(dimension_semantics=("parallel", "arbitrary"),
                                             vmem_limit_bytes=V7X_VMEM_LIMIT_BYTES),
        name="mla_causal_attention",
    )(q, kc)


def _dec_attn_kernel(pt_ref, q_ref, kn_ref, lat_hbm, pe_hbm, o_ref, *scratch, layer, n_pages):
    g = pl.program_id(0)
    lat_bufs = scratch[:DEC_SEQ_PER_STEP]
    pe_bufs = scratch[DEC_SEQ_PER_STEP:2 * DEC_SEQ_PER_STEP]
    kb_scrs = scratch[2 * DEC_SEQ_PER_STEP:2 * DEC_SEQ_PER_STEP + DEC_GROUP]
    sems = scratch[-1]
    past = n_pages * PAGE_SIZE

    def page_copies(seq, buf, page):
        src = pt_ref[seq * n_pages + page]
        dst = pl.ds(page * PAGE_SIZE, PAGE_SIZE)
        return (pltpu.make_async_copy(lat_hbm.at[layer, src], lat_bufs[buf].at[dst, :], sems.at[0, buf]),
                pltpu.make_async_copy(pe_hbm.at[layer, src], pe_bufs[buf].at[:, dst], sems.at[1, buf]))

    def start_seq(seq, buf):
        for page in range(n_pages):
            for cp in page_copies(seq, buf, page):
                cp.start(priority=page % 2)

    def wait_seq(seq, buf):
        for page in range(n_pages):
            for cp in page_copies(seq, buf, page):
                cp.wait()

    chunks = [slice(c, c + DEC_KEY_CHUNK) for c in range(0, past, DEC_KEY_CHUNK)]

    def attend(bufs):
        n = len(bufs)
        q = [q_ref[b] for b in bufs]
        kn = [kn_ref[b].astype(F32) for b in bufs]
        parts = [[] for _ in bufs]
        for rows in chunks:
            for i, b in enumerate(bufs):
                lat = lat_bufs[b][rows, :].astype(BF16)
                kb_scrs[i][rows, :] = lat
                parts[i].append(_dot_nt(q[i][:, :KV_LORA_RANK], lat)
                                + _dot(q[i][:, KV_LORA_RANK:KV_LORA_RANK + QK_ROPE_DIM],
                                       pe_bufs[b][:, rows].astype(BF16)))
        pb, o, denom = [], [], []
        for i in range(n):
            s = jnp.concatenate(parts[i], axis=1)
            s_new = jnp.sum(q[i].astype(F32) * kn[i], axis=-1, keepdims=True)
            m = jnp.maximum(jnp.max(s, axis=-1, keepdims=True), s_new)
            p = jnp.exp2(s - m)
            p_new = jnp.exp2(s_new - m)
            denom.append(jnp.sum(p, axis=-1, keepdims=True) + p_new)
            pb.append(p.astype(BF16))
            o.append(p_new * kn[i][:, :KV_LORA_RANK])
        for rows in chunks:
            for i in range(n):
                o[i] = o[i] + _dot(pb[i][:, rows], kb_scrs[i][rows, :])
        for i, b in enumerate(bufs):
            o_ref[b] = (o[i] / denom[i]).astype(BF16)

    groups = [tuple(range(k, k + DEC_GROUP)) for k in range(0, DEC_SEQ_PER_STEP, DEC_GROUP)]

    @pl.when(g == 0)
    def _():
        for k in range(DEC_SEQ_PER_STEP):
            start_seq(k, k)

    for bufs in groups:
        for b in bufs:
            wait_seq(DEC_SEQ_PER_STEP * g + b, b)
        attend(bufs)

        @pl.when(g + 1 < pl.num_programs(0))
        def _():
            for b in bufs:
                start_seq(DEC_SEQ_PER_STEP * (g + 1) + b, b)


def _decode_attention(q, kn, cache_lat, cache_pe_t, page_table, layer):
    n_seq, n_pages = page_table.shape
    past = n_pages * PAGE_SIZE
    n_buf = DEC_SEQ_PER_STEP
    assert n_seq % n_buf == 0 and past % DEC_KEY_CHUNK == 0 and n_buf % DEC_GROUP == 0 and n_buf > DEC_GROUP
    grid_spec = pltpu.PrefetchScalarGridSpec(
        num_scalar_prefetch=1,
        grid=(n_seq // n_buf,),
        in_specs=[pl.BlockSpec((n_buf, DEC_HEAD_ROWS, QK_DIM), lambda g, pt: (g, 0, 0)),
                  pl.BlockSpec((n_buf, 1, QK_DIM), lambda g, pt: (g, 0, 0)),
                  pl.BlockSpec(memory_space=pl.ANY),
                  pl.BlockSpec(memory_space=pl.ANY)],
        out_specs=pl.BlockSpec((n_buf, DEC_HEAD_ROWS, KV_LORA_RANK), lambda g, pt: (g, 0, 0)),
        scratch_shapes=([pltpu.VMEM((past, KV_LORA_RANK), F32)] * n_buf
                        + [pltpu.VMEM((QK_ROPE_DIM, past), F32)] * n_buf
                        + [pltpu.VMEM((past, KV_LORA_RANK), BF16)] * DEC_GROUP
                        + [pltpu.SemaphoreType.DMA((2, n_buf))]),
    )
    return pl.pallas_call(
        functools.partial(_dec_attn_kernel, layer=layer, n_pages=n_pages),
        out_shape=jax.ShapeDtypeStruct((n_seq, DEC_HEAD_ROWS, KV_LORA_RANK), BF16),
        grid_spec=grid_spec,
        compiler_params=pltpu.CompilerParams(dimension_semantics=("arbitrary",),
                                             vmem_limit_bytes=V7X_VMEM_LIMIT_BYTES),
        name="mla_paged_decode_attention",
    )(page_table.reshape(-1), q, kn, cache_lat, cache_pe_t)


def _mla_out_kernel(ol_ref, x_ref, mod_ref, npost_ref, wv_ref, wo_ref, xo_ref, ov_scr, *, seq_mode):
    gate = _mod_get(mod_ref, 2, seq_mode)
    for hd in range(N_HEADS):
        ov_scr[:, hd * V_HEAD_DIM:(hd + 1) * V_HEAD_DIM] = _dot(ol_ref[hd], wv_ref[hd]).astype(BF16)
    o = _dot(ov_scr[...], wo_ref[...])
    xo_ref[...] = x_ref[...] + _rms(o, npost_ref[...] * gate)


def _mla_out(o_lat, x, mod, npost, w_v, w_o, *, seq_mode):
    nb, length, _ = x.shape
    tile = TOKEN_TILE if seq_mode else length
    x_spec = pl.BlockSpec((None, tile, D_MODEL), lambda b, t: (b, t, 0))
    return pl.pallas_call(
        functools.partial(_mla_out_kernel, seq_mode=seq_mode),
        out_shape=jax.ShapeDtypeStruct(x.shape, F32),
        grid=(nb, length // tile),
        in_specs=[pl.BlockSpec((None, N_HEADS, tile, KV_LORA_RANK), lambda b, t: (b, 0, t, 0)),
                  x_spec, _mod_specs(seq_mode, tile), _resident((1, D_MODEL)),
                  _resident(w_v.shape), _weight_spec(w_o)],
        out_specs=x_spec,
        scratch_shapes=[pltpu.VMEM((tile, N_HEADS * V_HEAD_DIM), BF16)],
        compiler_params=pltpu.CompilerParams(dimension_semantics=("parallel", "parallel"),
                                             vmem_limit_bytes=V7X_VMEM_LIMIT_BYTES),
        name=f"mla_out_{'seq' if seq_mode else 'tok'}",
    )(o_lat, x, mod, npost.reshape(1, -1), w_v, _weight_array(w_o))


def _pad_lanes(w, width):
    return jnp.pad(w, [(0, 0)] * (w.ndim - 1) + [(0, width - w.shape[-1])])


def _swap_halves(w):
    return jnp.concatenate([w[..., ROPE_HALF:], w[..., :ROPE_HALF]], axis=-1)


def _prep_mla_weights(w_qb, w_kva, w_kvb):
    qb = w_qb.reshape(Q_LORA_RANK, N_HEADS, QK_NOPE_DIM + QK_ROPE_DIM)
    q_nope = qb[..., :QK_NOPE_DIM].reshape(Q_LORA_RANK, -1)
    q_pe = qb[..., QK_NOPE_DIM:]
    w_qb2 = jnp.concatenate([q_nope,
                             _pad_lanes(q_pe, ROPE_PAD).reshape(Q_LORA_RANK, -1),
                             _pad_lanes(_swap_halves(q_pe), ROPE_PAD).reshape(Q_LORA_RANK, -1)], axis=-1)
    k_pe = w_kva[:, KV_LORA_RANK:]
    w_kva2 = jnp.concatenate([w_kva[:, :KV_LORA_RANK], _pad_lanes(k_pe, ROPE_PAD),
                              _pad_lanes(_swap_halves(k_pe), ROPE_PAD)], axis=-1)
    kvb = w_kvb.reshape(KV_LORA_RANK, N_HEADS, QK_NOPE_DIM + V_HEAD_DIM)
    w_kt = jnp.transpose(kvb[..., :QK_NOPE_DIM], (1, 2, 0))
    w_v = jnp.transpose(kvb[..., QK_NOPE_DIM:], (1, 0, 2))
    return w_qb2.astype(BF16), w_kva2.astype(BF16), w_kt.astype(BF16), w_v.astype(BF16)


def _rope_tables(pos):
    inv = ROPE_THETA ** (-jnp.arange(ROPE_HALF, dtype=F32) * 2.0 / QK_ROPE_DIM)
    ang = pos.astype(F32)[:, None] * inv[None, :]
    cos, sin = jnp.cos(ang), jnp.sin(ang)
    return (_pad_lanes(jnp.concatenate([cos, cos], axis=-1), ROPE_PAD),
            _pad_lanes(jnp.concatenate([-sin, sin], axis=-1), ROPE_PAD))


def _trunk(x, mods, pos, mix_prev, ffn_prev, attn_past, w, *, seq_mode):
    depth = len(mods)
    cos, sin = _rope_tables(pos)
    new_lat, new_pe, new_mix, new_ffn = [], [], [], []
    for i in range(depth):
        j = i // 2
        mod = mods[i]
        if i % 2 == 0:
            x, st = _gmlp(x, mod, w["norm_mix_pre"][i], w["norm_mix_post"][i], (w["sc_w_in"], j), w["sc_conv_w"][j],
                          jnp.zeros((D_MODEL,), F32), (w["sc_w_out"], j),
                          None if seq_mode else mix_prev[j], kind="mixer", seq_mode=seq_mode)
            new_mix.append(st if seq_mode else jnp.stack([mix_prev[j][:, 1], st], axis=1))
        else:
            q, kc, lat, pe = _mla_proj(x, mod, w["norm_mix_pre"][i], (w["mla_w_qa"], j), w["mla_qa_norm"][j],
                                       w["mla_w_qb"][j], w["mla_w_kva"][j], w["mla_kv_norm"][j], w["mla_w_kt"][j],
                                       cos, sin, seq_mode=seq_mode)
            if seq_mode:
                o_lat = _attention(q, kc)
            else:
                cache_lat, cache_pe, page_table = attn_past
                qs = jnp.pad(jnp.transpose(q[0], (1, 0, 2)), ((0, 0), (0, DEC_HEAD_ROWS - N_HEADS), (0, 0)))
                o = _decode_attention(qs, jnp.transpose(kc, (1, 0, 2)), cache_lat, cache_pe, page_table, j)
                o_lat = jnp.transpose(o[:, :N_HEADS], (1, 0, 2))[None]
            x = _mla_out(o_lat, x, mod, w["norm_mix_post"][i], w["mla_w_v"][j], (w["mla_w_o"], j), seq_mode=seq_mode)
            new_lat.append(lat)
            new_pe.append(pe)
        x, st = _gmlp(x, mod, w["norm_ffn_pre"][i], w["norm_ffn_post"][i], (w["ffn_w_in"], i), w["ffn_conv_w"][i],
                      w["ffn_conv_b"][i], (w["ffn_w_out"], i),
                      None if seq_mode else ffn_prev[i], kind="ffn", seq_mode=seq_mode)
        new_ffn.append(st if seq_mode else jnp.stack([ffn_prev[i][:, 1], st], axis=1))
    return x, jnp.stack(new_lat), jnp.stack(new_pe), jnp.stack(new_mix), jnp.stack(new_ffn)


def kernel(x_prompt, x_sample, cache_kv_latent, cache_k_rope, state_mixconv, state_ffnconv, page_table, c_prompt, c_sample, mod_w, mod_b, norm_mix_pre, norm_mix_post, norm_ffn_pre, norm_ffn_post, sc_w_in, sc_conv_w, sc_w_out, mla_w_qa, mla_qa_norm, mla_w_qb, mla_w_kva, mla_kv_norm, mla_w_kvb, mla_w_o, ffn_w_in, ffn_conv_w, ffn_conv_b, ffn_w_out):
    depth = mod_w.shape[0]
    n_attn = mla_w_qb.shape[0]
    n_prompt, seq_len, _ = x_prompt.shape
    n_sample, dec_len, _ = x_sample.shape
    assert dec_len == 1 and seq_len % TOKEN_TILE == 0 and seq_len % ATTN_T == 0

    prepped = [_prep_mla_weights(mla_w_qb[j], mla_w_kva[j], mla_w_kvb[j]) for j in range(n_attn)]
    w = {
        "norm_mix_pre": norm_mix_pre, "norm_mix_post": norm_mix_post,
        "norm_ffn_pre": norm_ffn_pre, "norm_ffn_post": norm_ffn_post,
        "sc_w_in": sc_w_in.astype(BF16), "sc_conv_w": sc_conv_w, "sc_w_out": sc_w_out.astype(BF16),
        "mla_w_qa": mla_w_qa.astype(BF16), "mla_qa_norm": mla_qa_norm, "mla_kv_norm": mla_kv_norm,
        "mla_w_qb": [p[0] for p in prepped], "mla_w_kva": [p[1] for p in prepped],
        "mla_w_kt": [p[2] for p in prepped], "mla_w_v": [p[3] for p in prepped],
        "mla_w_o": mla_w_o.astype(BF16),
        "ffn_w_in": ffn_w_in.astype(BF16), "ffn_conv_w": ffn_conv_w, "ffn_conv_b": ffn_conv_b,
        "ffn_w_out": ffn_w_out.astype(BF16),
    }

    mod_all = _modulation(jnp.concatenate([c_sample, c_prompt], axis=0), mod_w, mod_b)
    mods_s = [mod_all[i, :n_sample] for i in range(depth)]
    mods_p = [mod_all[i, n_sample:].reshape(n_prompt, N_MOD, D_MODEL) for i in range(depth)]

    pos_p = jnp.arange(seq_len, dtype=jnp.int32)
    y_p, lat_p, pe_p, mix_p, ffn_p = _trunk(x_prompt, mods_p, pos_p, None, None, None, w, seq_mode=True)

    past_len = page_table.shape[1] * cache_kv_latent.shape[2]
    pos_s = jnp.full((n_sample,), past_len, dtype=jnp.int32)
    xs = x_sample.reshape(1, n_sample, D_MODEL)
    y_s, lat_s, pe_s, mix_s, ffn_s = _trunk(xs, mods_s, pos_s, state_mixconv, state_ffnconv,
                                            (cache_kv_latent, jnp.swapaxes(cache_k_rope, 2, 3), page_table), w,
                                            seq_mode=False)

    return (y_p, y_s.reshape(n_sample, 1, D_MODEL), lat_p, pe_p,
            lat_s.reshape(n_attn, n_sample, 1, KV_LORA_RANK), pe_s.reshape(n_attn, n_sample, 1, QK_ROPE_DIM),
            mix_p, mix_s, ffn_p, ffn_s)
```

```python
import functools

import jax
import jax.numpy as jnp
from jax import lax
from jax.experimental import pallas as pl
from jax.experimental.pallas import tpu as pltpu

F32 = jnp.float32
BF16 = jnp.bfloat16

D_MODEL = 1024
N_HEADS = 8
QK_NOPE_DIM = 128
QK_ROPE_DIM = 64
ROPE_HALF = QK_ROPE_DIM // 2
V_HEAD_DIM = 128
Q_LORA_RANK = 384
KV_LORA_RANK = 256
ROPE_PAD = 128
QK_DIM = KV_LORA_RANK + ROPE_PAD
D_FF = 2816
ROPE_THETA = 10000.0
RMS_EPS = 1e-6
N_MOD = 6
ATTN_SCALE = (QK_NOPE_DIM + QK_ROPE_DIM) ** -0.5
LOG2_E = 1.4426950408889634
QUERY_SCALE = ATTN_SCALE * LOG2_E
PAGE_SIZE = 128
LANES = 128
SUBLANES = 8
NEG_BIG = float(jnp.finfo(jnp.float32).min)

V7X_VMEM_LIMIT_BYTES = 56 * 1024 * 1024
TOKEN_TILE = 1024
HIDDEN_CHUNK = 256
ATTN_T = 256
ATTN_ROW_BLOCK = 64
DEC_SEQ_PER_STEP = 4
DEC_GROUP = 2
DEC_KEY_CHUNK = 1024
DEC_HEAD_ROWS = 16


def _dot(a, b):
    return jnp.dot(a, b, preferred_element_type=F32)


def _dot_nt(a, b):
    return lax.dot_general(a, b, (((1,), (1,)), ((), ())), preferred_element_type=F32)


def _rms(x, g):
    return x * lax.rsqrt(jnp.mean(x * x, axis=-1, keepdims=True) + RMS_EPS) * g


def _silu(x):
    return x / (1.0 + jnp.exp(-x))


def _resident(shape):
    return pl.BlockSpec(shape, lambda *_: (0,) * len(shape), pipeline_mode=pl.Buffered(1))


def _weight_shape(w):
    return w[0].shape[1:] if isinstance(w, tuple) else w.shape


def _weight_array(w):
    return w[0] if isinstance(w, tuple) else w


def _weight_spec(w):
    if not isinstance(w, tuple):
        return _resident(w.shape)
    stack, layer = w
    shape = stack.shape[1:]
    return pl.BlockSpec((None,) + shape, lambda *_: (layer,) + (0,) * len(shape), pipeline_mode=pl.Buffered(1))


def _mod_specs(seq_mode, tile):
    if seq_mode:
        return pl.BlockSpec((None, N_MOD, D_MODEL), lambda b, t: (b, 0, 0))
    return pl.BlockSpec((tile, N_MOD * D_MODEL), lambda b, t: (t, 0))


def _mod_get(mod_ref, k, seq_mode):
    if seq_mode:
        return mod_ref[k:k + 1, :]
    return mod_ref[:, k * D_MODEL:(k + 1) * D_MODEL]


def _mod_kernel(c_ref, w_ref, b_ref, o_ref):
    a = _silu(c_ref[...]).astype(BF16)
    o_ref[...] = _dot(a, w_ref[...].astype(BF16)) + b_ref[...]


def _modulation(c_all, mod_w, mod_b):
    depth, _, n_out = mod_w.shape
    rows = c_all.shape[0]
    tn = 1536
    return pl.pallas_call(
        _mod_kernel,
        out_shape=jax.ShapeDtypeStruct((depth, rows, n_out), F32),
        grid=(depth, n_out // tn),
        in_specs=[pl.BlockSpec((rows, D_MODEL), lambda l, n: (0, 0)),
                  pl.BlockSpec((None, D_MODEL, tn), lambda l, n: (l, 0, n)),
                  pl.BlockSpec((None, 1, tn), lambda l, n: (l, 0, n))],
        out_specs=pl.BlockSpec((None, rows, tn), lambda l, n: (l, 0, n)),
        compiler_params=pltpu.CompilerParams(dimension_semantics=("parallel", "parallel"),
                                             vmem_limit_bytes=V7X_VMEM_LIMIT_BYTES),
        name="adaln_modulation",
    )(c_all, mod_w, mod_b.reshape(depth, 1, n_out))


def _gmlp_kernel(*refs, kind, seq_mode, tile, hidden):
    if seq_mode:
        (x_ref, mod_ref, npre_ref, npost_ref, win_ref, cw_ref, cb_ref, wout_ref,
         xo_ref, st_ref, h_scr, g_scr, pre_scr, halo_scr) = refs
    else:
        (x_ref, mod_ref, npre_ref, npost_ref, win_ref, cw_ref, cb_ref, wout_ref, p0_ref, p1_ref,
         xo_ref, st_ref, h_scr, g_scr, pre_scr) = refs
    mo = 0 if kind == "mixer" else 3
    shift, scale, gate = (_mod_get(mod_ref, mo + k, seq_mode) for k in range(3))

    x = x_ref[...]
    h_scr[...] = (_rms(x, npre_ref[...] * (1.0 + scale)) + shift).astype(BF16)

    if seq_mode:
        @pl.when(pl.program_id(1) == 0)
        def _():
            halo_scr[...] = jnp.zeros_like(halo_scr)
        top_row = lax.broadcasted_iota(jnp.int32, (SUBLANES, 1), 0)

    ck = HIDDEN_CHUNK
    n_gates = 3 if kind == "mixer" else 2

    def project_in(c, slot):
        hb = h_scr[...]
        for gi in range(n_gates):
            lo = gi * hidden + c * ck
            pre_scr[slot, gi] = _dot(hb, win_ref[:, lo:lo + ck])

    project_in(0, 0)
    for c in range(hidden // ck):
        lo = c * ck
        slot = c % 2
        if c + 1 < hidden // ck:
            project_in(c + 1, 1 - slot)
        if kind == "mixer":
            bg = pre_scr[slot, 0]
            u = pre_scr[slot, 1] * pre_scr[slot, 2]
        else:
            u = pre_scr[slot, 0]
            vg = pre_scr[slot, 1]
        w0, w1, w2 = (cw_ref[k:k + 1, lo:lo + ck] for k in range(3))
        if seq_mode:
            u1 = pltpu.roll(u, 1, 0)
            u2 = pltpu.roll(u, 2, 0)
            y = u2 * w0 + u1 * w1 + u * w2
            h0 = halo_scr[0:1, lo:lo + ck]
            h1 = halo_scr[1:2, lo:lo + ck]
            u1_top = jnp.where(top_row == 0, h1, u1[:SUBLANES])
            u2_top = jnp.where(top_row == 0, h0, jnp.where(top_row == 1, h1, u2[:SUBLANES]))
            y_top = u2_top * w0 + u1_top * w1 + u[:SUBLANES] * w2
            y = jnp.concatenate([y_top, y[SUBLANES:]], axis=0)
            last2 = u[tile - 2:tile, :]
            halo_scr[:, lo:lo + ck] = last2
            st_ref[:, lo:lo + ck] = last2
        else:
            y = p0_ref[:, lo:lo + ck] * w0 + p1_ref[:, lo:lo + ck] * w1 + u * w2
            st_ref[:, lo:lo + ck] = u
        if kind == "mixer":
            g = bg * y
        else:
            g = _silu(y + cb_ref[:, lo:lo + ck]) * vg
        g_scr[:, lo:lo + ck] = g.astype(BF16)

    halves = [slice(0, tile // 2), slice(tile // 2, tile)] if seq_mode else [slice(0, tile)]
    outs = [_dot(g_scr[rows, :], wout_ref[...]) for rows in halves]
    for rows, o in zip(halves, outs):
        g_rows = gate if seq_mode else gate[rows, :]
        xo_ref[rows, :] = x_ref[rows, :] + _rms(o, npost_ref[...] * g_rows)


def _gmlp(x, mod, npre, npost, w_in, conv_w, conv_b, w_out, prev, *, kind, seq_mode):
    nb, length, _ = x.shape
    hidden = _weight_shape(w_out)[0]
    tile = TOKEN_TILE if seq_mode else length
    grid = (nb, length // tile)
    x_spec = pl.BlockSpec((None, tile, D_MODEL), lambda b, t: (b, t, 0))
    in_specs = [x_spec, _mod_specs(seq_mode, tile),
                _resident((1, D_MODEL)), _resident((1, D_MODEL)),
                _weight_spec(w_in), _resident((3, hidden)), _resident((1, hidden)), _weight_spec(w_out)]
    args = [x, mod, npre.reshape(1, -1), npost.reshape(1, -1), _weight_array(w_in), conv_w, conv_b.reshape(1, -1),
            _weight_array(w_out)]
    n_gates = 3 if kind == "mixer" else 2
    scratch = [pltpu.VMEM((tile, D_MODEL), BF16), pltpu.VMEM((tile, hidden), BF16),
               pltpu.VMEM((2, n_gates, tile, HIDDEN_CHUNK), F32)]
    if seq_mode:
        st_shape = jax.ShapeDtypeStruct((nb, 2, hidden), F32)
        st_spec = pl.BlockSpec((None, 2, hidden), lambda b, t: (b, 0, 0))
        scratch.append(pltpu.VMEM((2, hidden), F32))
    else:
        prev_spec = pl.BlockSpec((tile, hidden), lambda b, t: (t, 0))
        in_specs += [prev_spec, prev_spec]
        args += [prev[:, 0], prev[:, 1]]
        st_shape = jax.ShapeDtypeStruct((length, hidden), F32)
        st_spec = pl.BlockSpec((tile, hidden), lambda b, t: (t, 0))
    return pl.pallas_call(
        functools.partial(_gmlp_kernel, kind=kind, seq_mode=seq_mode, tile=tile, hidden=hidden),
        out_shape=(jax.ShapeDtypeStruct(x.shape, F32), st_shape),
        grid=grid,
        in_specs=in_specs,
        out_specs=(x_spec, st_spec),
        scratch_shapes=scratch,
        compiler_params=pltpu.CompilerParams(dimension_semantics=("parallel", "arbitrary"),
                                             vmem_limit_bytes=V7X_VMEM_LIMIT_BYTES),
        name=f"gated_conv_mlp_{kind}_{'seq' if seq_mode else 'tok'}",
    )(*args)


def _mla_proj_kernel(x_ref, mod_ref, npre_ref, wqa_ref, qan_ref, wqb_ref, wkva_ref, kvn_ref, wkt_ref,
                     cos_ref, sin_ref, q_out, kc_out, lat_out, pe_out, *, seq_mode):
    shift, scale = (_mod_get(mod_ref, k, seq_mode) for k in range(2))
    h = (_rms(x_ref[...], npre_ref[...] * (1.0 + scale)) + shift).astype(BF16)
    cos = cos_ref[...]
    sin = sin_ref[...]

    kv = _dot(h, wkva_ref[...])
    lat = _rms(kv[:, :KV_LORA_RANK], kvn_ref[...])
    pe = kv[:, KV_LORA_RANK:KV_LORA_RANK + ROPE_PAD] * cos + kv[:, KV_LORA_RANK + ROPE_PAD:] * sin
    lat_out[...] = lat
    pe_out[...] = pe[:, :QK_ROPE_DIM]
    kc_out[:, :KV_LORA_RANK] = lat.astype(BF16)
    kc_out[:, KV_LORA_RANK:] = pe.astype(BF16)

    qa = _rms(_dot(h, wqa_ref[...]), qan_ref[...]).astype(BF16)
    q = _dot(qa, wqb_ref[...])
    nope_w = N_HEADS * QK_NOPE_DIM
    pe_w = N_HEADS * ROPE_PAD
    for hd in range(N_HEADS):
        qn = q[:, hd * QK_NOPE_DIM:(hd + 1) * QK_NOPE_DIM].astype(BF16)
        ql = _dot(qn, wkt_ref[hd]) * QUERY_SCALE
        qp = (q[:, nope_w + hd * ROPE_PAD:nope_w + (hd + 1) * ROPE_PAD] * cos
              + q[:, nope_w + pe_w + hd * ROPE_PAD:nope_w + pe_w + (hd + 1) * ROPE_PAD] * sin) * QUERY_SCALE
        q_out[hd, :, :KV_LORA_RANK] = ql.astype(BF16)
        q_out[hd, :, KV_LORA_RANK:] = qp.astype(BF16)


def _mla_proj(x, mod, npre, w_qa, qa_norm, w_qb, w_kva, kv_norm, w_kt, cos, sin, *, seq_mode):
    nb, length, _ = x.shape
    tile = TOKEN_TILE if seq_mode else length
    row_spec = lambda w: pl.BlockSpec((None, tile, w), lambda b, t: (b, t, 0))
    tab_spec = pl.BlockSpec((tile, ROPE_PAD), lambda b, t: (t, 0))
    return pl.pallas_call(
        functools.partial(_mla_proj_kernel, seq_mode=seq_mode),
        out_shape=(jax.ShapeDtypeStruct((nb, N_HEADS, length, QK_DIM), BF16),
                   jax.ShapeDtypeStruct((nb, length, QK_DIM), BF16),
                   jax.ShapeDtypeStruct((nb, length, KV_LORA_RANK), F32),
                   jax.ShapeDtypeStruct((nb, length, QK_ROPE_DIM), F32)),
        grid=(nb, length // tile),
        in_specs=[row_spec(D_MODEL), _mod_specs(seq_mode, tile), _resident((1, D_MODEL)),
                  _weight_spec(w_qa), _resident((1, Q_LORA_RANK)), _resident(w_qb.shape),
                  _resident(w_kva.shape), _resident((1, KV_LORA_RANK)), _resident(w_kt.shape),
                  tab_spec, tab_spec],
        out_specs=(pl.BlockSpec((None, N_HEADS, tile, QK_DIM), lambda b, t: (b, 0, t, 0)),
                   row_spec(QK_DIM), row_spec(KV_LORA_RANK), row_spec(QK_ROPE_DIM)),
        compiler_params=pltpu.CompilerParams(dimension_semantics=("parallel", "parallel"),
                                             vmem_limit_bytes=V7X_VMEM_LIMIT_BYTES),
        name=f"mla_proj_{'seq' if seq_mode else 'tok'}",
    )(x, mod, npre.reshape(1, -1), _weight_array(w_qa), qa_norm.reshape(1, -1), w_qb, w_kva, kv_norm.reshape(1, -1), w_kt,
      cos, sin)


def _attn_kernel(q_ref, k_ref, o_ref, sa_scr, sb_scr, m_scr, l_scr, alpha_scr, p_scr, acc_scr):
    qi = pl.program_id(1)
    rows = N_HEADS * ATTN_T

    def key_rows(j):
        return pl.ds(pl.multiple_of(j * ATTN_T, ATTN_T), ATTN_T)

    def scores(j, s_scr):
        s_scr[...] = _dot_nt(q_ref[...].reshape(rows, QK_DIM), k_ref[key_rows(j), :])

    def update(j, s_scr, masked, first=False):
        v = k_ref[key_rows(j), :KV_LORA_RANK]
        stat_shape = (ATTN_ROW_BLOCK, LANES)
        for hd in range(N_HEADS):
            for tok0 in range(0, ATTN_T, ATTN_ROW_BLOCK):
                r = slice(hd * ATTN_T + tok0, hd * ATTN_T + tok0 + ATTN_ROW_BLOCK)
                s = s_scr[r, :]
                if masked:
                    tok = tok0 + lax.broadcasted_iota(jnp.int32, (ATTN_ROW_BLOCK, 1), 0)
                    key = lax.broadcasted_iota(jnp.int32, (1, ATTN_T), 1)
                    s = jnp.where(key <= tok, s, NEG_BIG)
                row_max = jnp.max(s, axis=-1, keepdims=True)
                if first:
                    m_new = jnp.broadcast_to(row_max, stat_shape)
                else:
                    m_prev = m_scr[r, :]
                    m_new = jnp.maximum(m_prev, row_max)
                    alpha = jnp.exp2(m_prev - m_new)
                    alpha_scr[r, :] = alpha
                p = jnp.exp2(s - jnp.tile(m_new, (1, ATTN_T // LANES)))
                row_sum = jnp.sum(p, axis=-1, keepdims=True)
                l_scr[r, :] = jnp.broadcast_to(row_sum, stat_shape) if first else alpha * l_scr[r, :] + row_sum
                m_scr[r, :] = m_new
                p_scr[r, :] = p.astype(BF16)
        pv = _dot(p_scr[...], v)
        if first:
            acc_scr[...] = pv
        else:
            acc_scr[...] = jnp.tile(alpha_scr[...], (1, KV_LORA_RANK // LANES)) * acc_scr[...] + pv

    @pl.when(qi == 0)
    def _():
        scores(0, sa_scr)
        update(0, sa_scr, True, first=True)

    @pl.when(qi == 1)
    def _():
        scores(0, sa_scr)
        scores(1, sb_scr)
        update(0, sa_scr, False, first=True)
        update(1, sb_scr, True)

    @pl.when(qi >= 2)
    def _():
        scores(0, sa_scr)
        scores(1, sb_scr)
        update(0, sa_scr, False, first=True)
        scores(2, sa_scr)
        update(1, sb_scr, False)

    def pair(t, carry):
        scores(2 * t + 1, sb_scr)
        update(2 * t, sa_scr, False)
        scores(2 * t + 2, sa_scr)
        update(2 * t + 1, sb_scr, False)
        return carry
    lax.fori_loop(1, qi // 2, pair, 0)

    @pl.when(jnp.logical_and(qi >= 2, qi % 2 == 0))
    def _():
        update(qi, sa_scr, True)

    @pl.when(jnp.logical_and(qi >= 2, qi % 2 == 1))
    def _():
        scores(qi, sb_scr)
        update(qi - 1, sa_scr, False)
        update(qi, sb_scr, True)

    for hd in range(N_HEADS):
        r = slice(hd * ATTN_T, (hd + 1) * ATTN_T)
        o_ref[hd] = (acc_scr[r, :] / jnp.tile(l_scr[r, :], (1, KV_LORA_RANK // LANES))).astype(BF16)


def _attention(q, kc):
    nb, _, length, _ = q.shape
    rows = N_HEADS * ATTN_T
    return pl.pallas_call(
        _attn_kernel,
        out_shape=jax.ShapeDtypeStruct((nb, N_HEADS, length, KV_LORA_RANK), BF16),
        grid=(nb, length // ATTN_T),
        in_specs=[pl.BlockSpec((None, N_HEADS, ATTN_T, QK_DIM), lambda b, i: (b, 0, i, 0)),
                  pl.BlockSpec((None, length, QK_DIM), lambda b, i: (b, 0, 0))],
        out_specs=pl.BlockSpec((None, N_HEADS, ATTN_T, KV_LORA_RANK), lambda b, i: (b, 0, i, 0)),
        scratch_shapes=[pltpu.VMEM((rows, ATTN_T), F32), pltpu.VMEM((rows, ATTN_T), F32),
                        pltpu.VMEM((rows, LANES), F32), pltpu.VMEM((rows, LANES), F32),
                        pltpu.VMEM((rows, LANES), F32), pltpu.VMEM((rows, ATTN_T), BF16),
                        pltpu.VMEM((rows, KV_LORA_RANK), F32)],
        compiler_params=pltpu.CompilerParams(dimension_semantics=("parallel", "arbitrary"),
                                             vmem_limit_bytes=V7X_VMEM_LIMIT_BYTES),
        name="mla_causal_attention",
    )(q, kc)


def _dec_attn_kernel(pt_ref, q_ref, kn_ref, lat_hbm, pe_hbm, o_ref, *scratch, layer, n_pages):
    g = pl.program_id(0)
    lat_bufs = scratch[:DEC_SEQ_PER_STEP]
    pe_bufs = scratch[DEC_SEQ_PER_STEP:2 * DEC_SEQ_PER_STEP]
    kb_scrs = scratch[2 * DEC_SEQ_PER_STEP:2 * DEC_SEQ_PER_STEP + DEC_GROUP]
    sems = scratch[-1]
    past = n_pages * PAGE_SIZE

    def page_copies(seq, buf, page):
        src = pt_ref[seq * n_pages + page]
        dst = pl.ds(page * PAGE_SIZE, PAGE_SIZE)
        return (pltpu.make_async_copy(lat_hbm.at[layer, src], lat_bufs[buf].at[dst, :], sems.at[0, buf]),
                pltpu.make_async_copy(pe_hbm.at[layer, src], pe_bufs[buf].at[:, dst], sems.at[1, buf]))

    def start_seq(seq, buf):
        for page in range(n_pages):
            for cp in page_copies(seq, buf, page):
                cp.start(priority=page % 2)

    def wait_seq(seq, buf):
        for page in range(n_pages):
            for cp in page_copies(seq, buf, page):
                cp.wait()

    chunks = [slice(c, c + DEC_KEY_CHUNK) for c in range(0, past, DEC_KEY_CHUNK)]

    def attend(bufs):
        n = len(bufs)
        q = [q_ref[b] for b in bufs]
        kn = [kn_ref[b].astype(F32) for b in bufs]
        parts = [[] for _ in bufs]
        for rows in chunks:
            for i, b in enumerate(bufs):
                lat = lat_bufs[b][rows, :].astype(BF16)
                kb_scrs[i][rows, :] = lat
                parts[i].append(_dot_nt(q[i][:, :KV_LORA_RANK], lat)
                                + _dot(q[i][:, KV_LORA_RANK:KV_LORA_RANK + QK_ROPE_DIM],
                                       pe_bufs[b][:, rows].astype(BF16)))
        pb, o, denom = [], [], []
        for i in range(n):
            s = jnp.concatenate(parts[i], axis=1)
            s_new = jnp.sum(q[i].astype(F32) * kn[i], axis=-1, keepdims=True)
            m = jnp.maximum(jnp.max(s, axis=-1, keepdims=True), s_new)
            p = jnp.exp2(s - m)
            p_new = jnp.exp2(s_new - m)
            denom.append(jnp.sum(p, axis=-1, keepdims=True) + p_new)
            pb.append(p.astype(BF16))
            o.append(p_new * kn[i][:, :KV_LORA_RANK])
        for rows in chunks:
            for i in range(n):
                o[i] = o[i] + _dot(pb[i][:, rows], kb_scrs[i][rows, :])
        for i, b in enumerate(bufs):
            o_ref[b] = (o[i] / denom[i]).astype(BF16)

    groups = [tuple(range(k, k + DEC_GROUP)) for k in range(0, DEC_SEQ_PER_STEP, DEC_GROUP)]

    @pl.when(g == 0)
    def _():
        for k in range(DEC_SEQ_PER_STEP):
            start_seq(k, k)

    for bufs in groups:
        for b in bufs:
            wait_seq(DEC_SEQ_PER_STEP * g + b, b)
        attend(bufs)

        @pl.when(g + 1 < pl.num_programs(0))
        def _():
            for b in bufs:
                start_seq(DEC_SEQ_PER_STEP * (g + 1) + b, b)


def _decode_attention(q, kn, cache_lat, cache_pe_t, page_table, layer):
    n_seq, n_pages = page_table.shape
    past = n_pages * PAGE_SIZE
    n_buf = DEC_SEQ_PER_STEP
    assert n_seq % n_buf == 0 and past % DEC_KEY_CHUNK == 0 and n_buf % DEC_GROUP == 0 and n_buf > DEC_GROUP
    grid_spec = pltpu.PrefetchScalarGridSpec(
        num_scalar_prefetch=1,
        grid=(n_seq // n_buf,),
        in_specs=[pl.BlockSpec((n_buf, DEC_HEAD_ROWS, QK_DIM), lambda g, pt: (g, 0, 0)),
                  pl.BlockSpec((n_buf, 1, QK_DIM), lambda g, pt: (g, 0, 0)),
                  pl.BlockSpec(memory_space=pl.ANY),
                  pl.BlockSpec(memory_space=pl.ANY)],
        out_specs=pl.BlockSpec((n_buf, DEC_HEAD_ROWS, KV_LORA_RANK), lambda g, pt: (g, 0, 0)),
        scratch_shapes=([pltpu.VMEM((past, KV_LORA_RANK), F32)] * n_buf
                        + [pltpu.VMEM((QK_ROPE_DIM, past), F32)] * n_buf
                        + [pltpu.VMEM((past, KV_LORA_RANK), BF16)] * DEC_GROUP
                        + [pltpu.SemaphoreType.DMA((2, n_buf))]),
    )
    return pl.pallas_call(
        functools.partial(_dec_attn_kernel, layer=layer, n_pages=n_pages),
        out_shape=jax.ShapeDtypeStruct((n_seq, DEC_HEAD_ROWS, KV_LORA_RANK), BF16),
        grid_spec=grid_spec,
        compiler_params=pltpu.CompilerParams(dimension_semantics=("arbitrary",),
                                             vmem_limit_bytes=V7X_VMEM_LIMIT_BYTES),
        name="mla_paged_decode_attention",
    )(page_table.reshape(-1), q, kn, cache_lat, cache_pe_t)


def _mla_out_kernel(ol_ref, x_ref, mod_ref, npost_ref, wv_ref, wo_ref, xo_ref, ov_scr, *, seq_mode):
    gate = _mod_get(mod_ref, 2, seq_mode)
    for hd in range(N_HEADS):
        ov_scr[:, hd * V_HEAD_DIM:(hd + 1) * V_HEAD_DIM] = _dot(ol_ref[hd], wv_ref[hd]).astype(BF16)
    o = _dot(ov_scr[...], wo_ref[...])
    xo_ref[...] = x_ref[...] + _rms(o, npost_ref[...] * gate)


def _mla_out(o_lat, x, mod, npost, w_v, w_o, *, seq_mode):
    nb, length, _ = x.shape
    tile = TOKEN_TILE if seq_mode else length
    x_spec = pl.BlockSpec((None, tile, D_MODEL), lambda b, t: (b, t, 0))
    return pl.pallas_call(
        functools.partial(_mla_out_kernel, seq_mode=seq_mode),
        out_shape=jax.ShapeDtypeStruct(x.shape, F32),
        grid=(nb, length // tile),
        in_specs=[pl.BlockSpec((None, N_HEADS, tile, KV_LORA_RANK), lambda b, t: (b, 0, t, 0)),
                  x_spec, _mod_specs(seq_mode, tile), _resident((1, D_MODEL)),
                  _resident(w_v.shape), _weight_spec(w_o)],
        out_specs=x_spec,
        scratch_shapes=[pltpu.VMEM((tile, N_HEADS * V_HEAD_DIM), BF16)],
        compiler_params=pltpu.CompilerParams(dimension_semantics=("parallel", "parallel"),
                                             vmem_limit_bytes=V7X_VMEM_LIMIT_BYTES),
        name=f"mla_out_{'seq' if seq_mode else 'tok'}",
    )(o_lat, x, mod, npost.reshape(1, -1), w_v, _weight_array(w_o))


def _pad_lanes(w, width):
    return jnp.pad(w, [(0, 0)] * (w.ndim - 1) + [(0, width - w.shape[-1])])


def _swap_halves(w):
    return jnp.concatenate([w[..., ROPE_HALF:], w[..., :ROPE_HALF]], axis=-1)


def _prep_mla_weights(w_qb, w_kva, w_kvb):
    qb = w_qb.reshape(Q_LORA_RANK, N_HEADS, QK_NOPE_DIM + QK_ROPE_DIM)
    q_nope = qb[..., :QK_NOPE_DIM].reshape(Q_LORA_RANK, -1)
    q_pe = qb[..., QK_NOPE_DIM:]
    w_qb2 = jnp.concatenate([q_nope,
                             _pad_lanes(q_pe, ROPE_PAD).reshape(Q_LORA_RANK, -1),
                             _pad_lanes(_swap_halves(q_pe), ROPE_PAD).reshape(Q_LORA_RANK, -1)], axis=-1)
    k_pe = w_kva[:, KV_LORA_RANK:]
    w_kva2 = jnp.concatenate([w_kva[:, :KV_LORA_RANK], _pad_lanes(k_pe, ROPE_PAD),
                              _pad_lanes(_swap_halves(k_pe), ROPE_PAD)], axis=-1)
    kvb = w_kvb.reshape(KV_LORA_RANK, N_HEADS, QK_NOPE_DIM + V_HEAD_DIM)
    w_kt = jnp.transpose(kvb[..., :QK_NOPE_DIM], (1, 2, 0))
    w_v = jnp.transpose(kvb[..., QK_NOPE_DIM:], (1, 0, 2))
    return w_qb2.astype(BF16), w_kva2.astype(BF16), w_kt.astype(BF16), w_v.astype(BF16)


def _rope_tables(pos):
    inv = ROPE_THETA ** (-jnp.arange(ROPE_HALF, dtype=F32) * 2.0 / QK_ROPE_DIM)
    ang = pos.astype(F32)[:, None] * inv[None, :]
    cos, sin = jnp.cos(ang), jnp.sin(ang)
    return (_pad_lanes(jnp.concatenate([cos, cos], axis=-1), ROPE_PAD),
            _pad_lanes(jnp.concatenate([-sin, sin], axis=-1), ROPE_PAD))


def _trunk(x, mods, pos, mix_prev, ffn_prev, attn_past, w, *, seq_mode):
    depth = len(mods)
    cos, sin = _rope_tables(pos)
    new_lat, new_pe, new_mix, new_ffn = [], [], [], []
    for i in range(depth):
        j = i // 2
        mod = mods[i]
        if i % 2 == 0:
            x, st = _gmlp(x, mod, w["norm_mix_pre"][i], w["norm_mix_post"][i], (w["sc_w_in"], j), w["sc_conv_w"][j],
                          jnp.zeros((D_MODEL,), F32), (w["sc_w_out"], j),
                          None if seq_mode else mix_prev[j], kind="mixer", seq_mode=seq_mode)
            new_mix.append(st if seq_mode else jnp.stack([mix_prev[j][:, 1], st], axis=1))
        else:
            q, kc, lat, pe = _mla_proj(x, mod, w["norm_mix_pre"][i], (w["mla_w_qa"], j), w["mla_qa_norm"][j],
                                       w["mla_w_qb"][j], w["mla_w_kva"][j], w["mla_kv_norm"][j], w["mla_w_kt"][j],
                                       cos, sin, seq_mode=seq_mode)
            if seq_mode:
                o_lat = _attention(q, kc)
            else:
                cache_lat, cache_pe, page_table = attn_past
                qs = jnp.pad(jnp.transpose(q[0], (1, 0, 2)), ((0, 0), (0, DEC_HEAD_ROWS - N_HEADS), (0, 0)))
                o = _decode_attention(qs, jnp.transpose(kc, (1, 0, 2)), cache_lat, cache_pe, page_table, j)
                o_lat = jnp.transpose(o[:, :N_HEADS], (1, 0, 2))[None]
            x = _mla_out(o_lat, x, mod, w["norm_mix_post"][i], w["mla_w_v"][j], (w["mla_w_o"], j), seq_mode=seq_mode)
            new_lat.append(lat)
            new_pe.append(pe)
        x, st = _gmlp(x, mod, w["norm_ffn_pre"][i], w["norm_ffn_post"][i], (w["ffn_w_in"], i), w["ffn_conv_w"][i],
                      w["ffn_conv_b"][i], (w["ffn_w_out"], i),
                      None if seq_mode else ffn_prev[i], kind="ffn", seq_mode=seq_mode)
        new_ffn.append(st if seq_mode else jnp.stack([ffn_prev[i][:, 1], st], axis=1))
    return x, jnp.stack(new_lat), jnp.stack(new_pe), jnp.stack(new_mix), jnp.stack(new_ffn)


def kernel(x_prompt, x_sample, cache_kv_latent, cache_k_rope, state_mixconv, state_ffnconv, page_table, c_prompt, c_sample, mod_w, mod_b, norm_mix_pre, norm_mix_post, norm_ffn_pre, norm_ffn_post, sc_w_in, sc_conv_w, sc_w_out, mla_w_qa, mla_qa_norm, mla_w_qb, mla_w_kva, mla_kv_norm, mla_w_kvb, mla_w_o, ffn_w_in, ffn_conv_w, ffn_conv_b, ffn_w_out):
    depth = mod_w.shape[0]
    n_attn = mla_w_qb.shape[0]
    n_prompt, seq_len, _ = x_prompt.shape
    n_sample, dec_len, _ = x_sample.shape
    assert dec_len == 1 and seq_len % TOKEN_TILE == 0 and seq_len % ATTN_T == 0

    prepped = [_prep_mla_weights(mla_w_qb[j], mla_w_kva[j], mla_w_kvb[j]) for j in range(n_attn)]
    w = {
        "norm_mix_pre": norm_mix_pre, "norm_mix_post": norm_mix_post,
        "norm_ffn_pre": norm_ffn_pre, "norm_ffn_post": norm_ffn_post,
        "sc_w_in": sc_w_in.astype(BF16), "sc_conv_w": sc_conv_w, "sc_w_out": sc_w_out.astype(BF16),
        "mla_w_qa": mla_w_qa.astype(BF16), "mla_qa_norm": mla_qa_norm, "mla_kv_norm": mla_kv_norm,
        "mla_w_qb": [p[0] for p in prepped], "mla_w_kva": [p[1] for p in prepped],
        "mla_w_kt": [p[2] for p in prepped], "mla_w_v": [p[3] for p in prepped],
        "mla_w_o": mla_w_o.astype(BF16),
        "ffn_w_in": ffn_w_in.astype(BF16), "ffn_conv_w": ffn_conv_w, "ffn_conv_b": ffn_conv_b,
        "ffn_w_out": ffn_w_out.astype(BF16),
    }

    mod_all = _modulation(jnp.concatenate([c_sample, c_prompt], axis=0), mod_w, mod_b)
    mods_s = [mod_all[i, :n_sample] for i in range(depth)]
    mods_p = [mod_all[i, n_sample:].reshape(n_prompt, N_MOD, D_MODEL) for i in range(depth)]

    pos_p = jnp.arange(seq_len, dtype=jnp.int32)
    y_p, lat_p, pe_p, mix_p, ffn_p = _trunk(x_prompt, mods_p, pos_p, None, None, None, w, seq_mode=True)

    past_len = page_table.shape[1] * cache_kv_latent.shape[2]
    pos_s = jnp.full((n_sample,), past_len, dtype=jnp.int32)
    xs = x_sample.reshape(1, n_sample, D_MODEL)
    y_s, lat_s, pe_s, mix_s, ffn_s = _trunk(xs, mods_s, pos_s, state_mixconv, state_ffnconv,
                                            (cache_kv_latent, jnp.swapaxes(cache_k_rope, 2, 3), page_table), w,
                                            seq_mode=False)

    return (y_p, y_s.reshape(n_sample, 1, D_MODEL), lat_p, pe_p,
            lat_s.reshape(n_attn, n_sample, 1, KV_LORA_RANK), pe_s.reshape(n_attn, n_sample, 1, QK_ROPE_DIM),
            mix_p, mix_s, ffn_p, ffn_s)
```
